```python
import math
import jax, jax.numpy as jnp
from jax import lax
import numpy as np

D_MODEL = 1024
BATCH = 16
SEQ = 2048
DEPTH = 4

PLE_DIM = 256
EPS = 1e-6
BLOCK = 128
SB_HEADS = 8
SB_DIM = 64
SB_WIDTH = SB_HEADS * SB_DIM
HG_HEADS = 4
HG_DK = 128
HG_DV = 128
HG_QK = HG_HEADS * HG_DK
HG_V = HG_HEADS * HG_DV
HG_CHUNK = 64
AB_SPLITS = [SB_WIDTH] * 3 + [HG_QK] * 2 + [HG_V] * 2
AB_IN = sum(AB_SPLITS)
AB_OUT = SB_WIDTH + HG_V
SW_HEADS = 16
SW_KV_HEADS = 4
SW_DIM = 64
SW_GROUP = SW_HEADS // SW_KV_HEADS
WINDOW = 128
C_IN = (SW_HEADS + 2 * SW_KV_HEADS) * SW_DIM
C_OUT = SW_HEADS * SW_DIM
N_BUCKETS = 32
MAX_DISTANCE = 128
D_FF = 2816
CONV_W = 3
N_EVEN = (DEPTH + 1) // 2
N_ODD = DEPTH // 2

kernel_name = 'hybrid_sb_hgrn2_swa_convffn'


def rmsnorm(x, g):
    xf = x.astype(jnp.float32)
    y = xf * lax.rsqrt(jnp.mean(xf * xf, axis=-1, keepdims=True) + EPS)
    return (y * g.astype(jnp.float32)).astype(x.dtype)


def stick_breaking_attention(q, k, v):
    S = q.shape[1]
    qf = jnp.swapaxes(q, 1, 2).astype(jnp.float32)
    kf = jnp.swapaxes(k, 1, 2).astype(jnp.float32)
    vf = jnp.swapaxes(v, 1, 2).astype(jnp.float32)
    scale = SB_DIM ** -0.5
    outs = []
    for n in range(S // BLOCK):
        t0 = n * BLOCK
        kn = t0 + BLOCK
        z = jnp.einsum('bhtd,bhsd->bhts', qf[:, :, t0:kn], kf[:, :, :kn]) * scale
        t_pos = t0 + jnp.arange(BLOCK)[:, None]
        s_pos = jnp.arange(kn)[None, :]
        mask = s_pos < t_pos
        log_keep = jnp.where(mask, jax.nn.log_sigmoid(-z), 0.0)
        later = lax.cumsum(log_keep, axis=3, reverse=True) - log_keep
        w = jnp.where(mask, jnp.exp(jax.nn.log_sigmoid(z) + later), 0.0)
        outs.append(jnp.einsum('bhts,bhsd->bhtd', w, vf[:, :, :kn]))
    o = jnp.concatenate(outs, axis=2)
    return jnp.swapaxes(o, 1, 2)


def hgrn2(q, f_pre, i, lb):
    B, S = q.shape[:2]
    lb = lb.reshape(HG_HEADS, HG_DK).astype(jnp.float32)
    fp = f_pre.astype(jnp.float32)
    log_f = jnp.log(lb + (1.0 - lb) * jax.nn.sigmoid(fp))
    kk = (1.0 - lb) * jax.nn.sigmoid(-fp)
    qf = jax.nn.silu(q.astype(jnp.float32))
    nc = S // HG_CHUNK

    def to_chunks(a):
        return a.reshape(B, nc, HG_CHUNK, HG_HEADS, a.shape[-1]).transpose(1, 0, 3, 2, 4)

    causal = jnp.tril(jnp.ones((HG_CHUNK, HG_CHUNK), dtype=bool))

    def step(state, xs):
        qc, kc, lfc, ic = xs
        b = jnp.cumsum(lfc, axis=2)
        o_inter = jnp.einsum('bhtk,bhkv->bhtv', qc * jnp.exp(b), state)
        rel = jnp.where(causal[:, :, None], b[:, :, :, None, :] - b[:, :, None, :, :], -jnp.inf)
        scores = jnp.einsum('bhtk,bhsk,bhtsk->bhts', qc, kc, jnp.exp(rel))
        o = o_inter + jnp.einsum('bhts,bhsv->bhtv', scores, ic)
        b_last = b[:, :, -1:, :]
        state = jnp.exp(b_last[:, :, 0, :, None]) * state + jnp.einsum('bhsk,bhsv->bhkv', kc * jnp.exp(b_last - b), ic)
        return state, o

    s0 = jnp.zeros((B, HG_HEADS, HG_DK, HG_DV), jnp.float32)
    _, o = lax.scan(step, s0, (to_chunks(qf), to_chunks(kk), to_chunks(log_f), to_chunks(i.astype(jnp.float32))))
    return o.transpose(1, 0, 3, 2, 4).reshape(B, S, HG_HEADS, HG_DV)


def t5_band_buckets():
    t = np.arange(WINDOW)[:, None]
    s = np.arange(2 * WINDOW)[None, :]
    dist = t + WINDOW - s
    band = (dist >= 0) & (dist < WINDOW)
    max_exact = N_BUCKETS // 2
    large = max_exact + (np.log(np.maximum(dist, max_exact) / max_exact) / math.log(MAX_DISTANCE / max_exact) * (N_BUCKETS - max_exact)).astype(np.int32)
    large = np.minimum(large, N_BUCKETS - 1)
    bucket = np.where(dist < max_exact, np.maximum(dist, 0), large).astype(np.int32)
    return bucket, band


def sliding_window_attention(q, k, v, sinks, rel_bias):
    B, S = q.shape[:2]
    nb = S // WINDOW
    bucket, band = t5_band_buckets()
    bias = rel_bias.astype(jnp.float32)[bucket]
    bias = bias.transpose(2, 0, 1).reshape(SW_KV_HEADS, SW_GROUP, WINDOW, 2 * WINDOW)
    key_pos = np.arange(nb)[:, None] * WINDOW - WINDOW + np.arange(2 * WINDOW)[None, :]
    mask = jnp.asarray(band[None] & (key_pos >= 0)[:, None, :])
    qb = q.astype(jnp.float32).reshape(B, nb, WINDOW, SW_KV_HEADS, SW_GROUP, SW_DIM).transpose(1, 0, 2, 3, 4, 5)

    def band_keys(a):
        ap = jnp.pad(a.astype(jnp.float32), ((0, 0), (WINDOW, 0), (0, 0), (0, 0)))
        ap = ap.reshape(B, nb + 1, WINDOW, SW_KV_HEADS, SW_DIM)
        return jnp.concatenate([ap[:, :-1], ap[:, 1:]], axis=2).transpose(1, 0, 2, 3, 4)

    kb, vb = band_keys(k), band_keys(v)
    sink = sinks.astype(jnp.float32).reshape(SW_KV_HEADS, SW_GROUP, 1, 1)
    scale = SW_DIM ** -0.5

    def block_attn(args):
        qn, kn, vn, mn = args
        logits = jnp.einsum('bqhgd,bkhd->bhgqk', qn, kn) * scale + bias
        logits = jnp.where(mn, logits, -jnp.inf)
        m = jnp.maximum(jnp.max(logits, axis=-1, keepdims=True), sink)
        e = jnp.exp(logits - m)
        w = e / (jnp.sum(e, axis=-1, keepdims=True) + jnp.exp(sink - m))
        return jnp.einsum('bhgqk,bkhd->bqhgd', w, vn)

    o = lax.map(block_attn, (qb, kb, vb, mask))
    return o.transpose(1, 0, 2, 3, 4, 5).reshape(B, S, SW_HEADS, SW_DIM)


def mixer_ab(h, w_in, lb, hg_norm, w_out):
    B, S, _ = h.shape
    proj = h @ w_in
    qa, ka, va, qb, fb, ib, gb = jnp.split(proj, np.cumsum(AB_SPLITS)[:-1].tolist(), axis=-1)
    sb_shape = (B, S, SB_HEADS, SB_DIM)
    o_a = stick_breaking_attention(qa.reshape(sb_shape), ka.reshape(sb_shape), va.reshape(sb_shape))
    o_a = o_a.astype(h.dtype).reshape(B, S, SB_WIDTH)
    o_b = hgrn2(qb.reshape(B, S, HG_HEADS, HG_DK), fb.reshape(B, S, HG_HEADS, HG_DK), ib.reshape(B, S, HG_HEADS, HG_DV), lb)
    o_b = rmsnorm(o_b.astype(h.dtype), hg_norm) * jax.nn.silu(gb.reshape(B, S, HG_HEADS, HG_DV))
    o_b = o_b.reshape(B, S, HG_V)
    return jnp.concatenate([o_a, o_b], axis=-1) @ w_out


def mixer_c(h, w_in, q_norm, k_norm, sinks, rel_bias, w_out):
    B, S, _ = h.shape
    proj = h @ w_in
    q, k, v = jnp.split(proj, [SW_HEADS * SW_DIM, (SW_HEADS + SW_KV_HEADS) * SW_DIM], axis=-1)
    q = rmsnorm(q.reshape(B, S, SW_HEADS, SW_DIM), q_norm)
    k = rmsnorm(k.reshape(B, S, SW_KV_HEADS, SW_DIM), k_norm)
    v = v.reshape(B, S, SW_KV_HEADS, SW_DIM)
    o = sliding_window_attention(q, k, v, sinks, rel_bias).astype(h.dtype)
    return o.reshape(B, S, C_OUT) @ w_out


def conv_glu_ffn(h, w_up, conv_w, conv_b, w_down):
    u = h @ w_up
    u = lax.conv_general_dilated(u, conv_w[:, None, :], window_strides=(1,), padding=[(CONV_W - 1, 0)],
                                 dimension_numbers=('NWC', 'WIO', 'NWC'), feature_group_count=2 * D_FF) + conv_b
    gate, up = jnp.split(u, 2, axis=-1)
    return (jax.nn.silu(gate) * up) @ w_down


def setup_inputs(seed: int = 0) -> dict:
    key = jax.random.key(seed)
    ks = jax.random.split(key, 21)

    def nrm(k, shape):
        return jax.random.normal(k, shape, jnp.float32)

    def w(k, shape, fan_in):
        return nrm(k, shape) * fan_in ** -0.5

    def gain(k, shape):
        return 1.0 + 0.02 * nrm(k, shape)

    F2 = 2 * D_FF
    return {
        'x': nrm(ks[0], (BATCH, SEQ, D_MODEL)),
        'p': nrm(ks[1], (DEPTH, BATCH, SEQ, PLE_DIM)),
        'mix_norm': gain(ks[2], (DEPTH, D_MODEL)),
        'ab_w_in': w(ks[3], (N_EVEN, D_MODEL, AB_IN), D_MODEL),
        'hg_lb_logits': 0.5 * nrm(ks[4], (N_EVEN, HG_QK)),
        'hg_out_norm': gain(ks[5], (N_EVEN, HG_DV)),
        'ab_w_out': w(ks[6], (N_EVEN, AB_OUT, D_MODEL), AB_OUT),
        'c_w_in': w(ks[7], (N_ODD, D_MODEL, C_IN), D_MODEL),
        'q_norm': gain(ks[8], (N_ODD, SW_DIM)),
        'k_norm': gain(ks[9], (N_ODD, SW_DIM)),
        'sinks': 0.5 * nrm(ks[10], (N_ODD, SW_HEADS)),
        'rel_bias': 0.5 * nrm(ks[11], (N_BUCKETS, SW_HEADS)),
        'c_w_out': w(ks[12], (N_ODD, C_OUT, D_MODEL), C_OUT),
        'ffn_norm': gain(ks[13], (DEPTH, D_MODEL)),
        'ffn_up': w(ks[14], (DEPTH, D_MODEL, F2), D_MODEL),
        'ffn_conv': w(ks[15], (DEPTH, CONV_W, F2), CONV_W),
        'ffn_conv_b': 0.02 * nrm(ks[16], (DEPTH, F2)),
        'ffn_down': w(ks[17], (DEPTH, D_FF, D_MODEL), D_FF),
        'ple_norm': gain(ks[18], (DEPTH, D_MODEL)),
        'ple_gate': w(ks[19], (DEPTH, D_MODEL, D_MODEL), D_MODEL),
        'ple_proj': w(ks[20], (DEPTH, PLE_DIM, D_MODEL), PLE_DIM),
    }


def reference(x, p, mix_norm, ab_w_in, hg_lb_logits, hg_out_norm, ab_w_out, c_w_in, q_norm, k_norm,
              sinks, rel_bias, c_w_out, ffn_norm, ffn_up, ffn_conv, ffn_conv_b, ffn_down,
              ple_norm, ple_gate, ple_proj):
    lb_cum = jnp.cumsum(jax.nn.softmax(hg_lb_logits.astype(jnp.float32), axis=0), axis=0)
    lower_bounds = lb_cum - lb_cum[0]
    h = x
    for i in range(DEPTH):
        j = i // 2
        hn = rmsnorm(h, mix_norm[i])
        if i % 2 == 0:
            h = h + mixer_ab(hn, ab_w_in[j], lower_bounds[j], hg_out_norm[j], ab_w_out[j])
        else:
            h = h + mixer_c(hn, c_w_in[j], q_norm[j], k_norm[j], sinks[j], rel_bias, c_w_out[j])
        h = h + conv_glu_ffn(rmsnorm(h, ffn_norm[i]), ffn_up[i], ffn_conv[i], ffn_conv_b[i], ffn_down[i])
        gate = jax.nn.sigmoid(rmsnorm(h, ple_norm[i]) @ ple_gate[i])
        h = h + gate * (p[i] @ ple_proj[i])
    return h
```

```python
import functools
import math

import numpy as np
import jax
import jax.numpy as jnp
from jax import lax
from jax.experimental import pallas as pl
from jax.experimental.pallas import tpu as pltpu

F32 = jnp.float32
BF16 = jnp.bfloat16

D_MODEL = 1024
PLE_DIM = 256
EPS = 1e-6
SB_HEADS = 8
SB_DIM = 64
SB_WIDTH = SB_HEADS * SB_DIM
HG_HEADS = 4
HG_DK = 128
HG_DV = 128
HG_QK = HG_HEADS * HG_DK
HG_V = HG_HEADS * HG_DV
SW_HEADS = 16
SW_KV_HEADS = 4
SW_DIM = 64
SW_GROUP = SW_HEADS // SW_KV_HEADS
WINDOW = 128
N_BUCKETS = 32
MAX_DISTANCE = 128
D_FF = 2816
CONV_W = 3

LANES = 128
SUBLANES = 8
BLK = 128
HG_CHUNK = 16
FF_CHUNK = 256
VMEM_LIMIT = 56 * 1024 * 1024


def _cparams(sem):
    return pltpu.CompilerParams(dimension_semantics=sem, vmem_limit_bytes=VMEM_LIMIT)


def _rms(x, g):
    ms = jnp.mean(x * x, axis=-1, keepdims=True)
    return x * lax.rsqrt(ms + EPS) * g


def _dot(a, b):
    return jnp.dot(a, b, preferred_element_type=F32)


def _dot_nt(a, b):
    return lax.dot_general(a, b, (((1,), (1,)), ((), ())), preferred_element_type=F32)


def _dot_tn(a, b):
    return lax.dot_general(a, b, (((0,), (0,)), ((), ())), preferred_element_type=F32)


def _split_bf16(x):
    hi = x.astype(BF16)
    lo = (x - hi.astype(F32)).astype(BF16)
    return hi, lo


def _norm_proj_kernel(h_ref, g_ref, w_ref, *o_refs, splits, nchunk):
    hn = _rms(h_ref[...], g_ref[...]).astype(BF16)
    col = 0
    for o_ref, width in zip(o_refs, splits):
        for c0 in range(0, width, nchunk):
            cw = min(nchunk, width - c0)
            o_ref[:, c0:c0 + cw] = _dot(hn, w_ref[:, col + c0:col + c0 + cw]).astype(o_ref.dtype)
        col += width


def norm_proj(h, g, w, splits, dtypes, tm=512, nchunk=512):
    m, k = h.shape
    n = w.shape[1]
    assert sum(splits) == n and m % tm == 0
    return pl.pallas_call(
        functools.partial(_norm_proj_kernel, splits=tuple(splits), nchunk=nchunk),
        grid=(m // tm,),
        in_specs=[pl.BlockSpec((tm, k), lambda i: (i, 0)),
                  pl.BlockSpec((1, k), lambda i: (0, 0)),
                  pl.BlockSpec((k, n), lambda i: (0, 0))],
        out_specs=[pl.BlockSpec((tm, s), lambda i: (i, 0)) for s in splits],
        out_shape=[jax.ShapeDtypeStruct((m, s), d) for s, d in zip(splits, dtypes)],
        compiler_params=_cparams(("parallel",)),
        name="norm_proj",
    )(h, g.reshape(1, k), w)


def _proj_res_kernel(a_ref, w_ref, h_ref, o_ref):
    o_ref[...] = h_ref[...] + _dot(a_ref[...].astype(BF16), w_ref[...])


def proj_residual(a, w, h, tm=512):
    m, k = a.shape
    n = w.shape[1]
    return pl.pallas_call(
        _proj_res_kernel,
        grid=(m // tm,),
        in_specs=[pl.BlockSpec((tm, k), lambda i: (i, 0)),
                  pl.BlockSpec((k, n), lambda i: (0, 0)),
                  pl.BlockSpec((tm, n), lambda i: (i, 0))],
        out_specs=pl.BlockSpec((tm, n), lambda i: (i, 0)),
        out_shape=jax.ShapeDtypeStruct((m, n), F32),
        compiler_params=_cparams(("parallel",)),
        name="proj_residual",
    )(a, w, h)


def _ffn_kernel(h_ref, g_ref, wup_ref, cw_ref, cb_ref, wdn_ref, o_ref, hn_ref, act_ref, halo_ref,
                *, tiles_per_seq):
    tm = h_ref.shape[0]
    x = h_ref[...]
    hn_ref[...] = _rms(x, g_ref[...]).astype(BF16)
    seq_start = (pl.program_id(0) % tiles_per_seq) == 0
    row = lax.broadcasted_iota(jnp.int32, (tm, FF_CHUNK), 0)

    def conv_cols(c0):
        cols = slice(c0, c0 + FF_CHUNK)
        u = _dot(hn_ref[...], wup_ref[:, cols])
        prev = jnp.where(seq_start, 0.0, halo_ref[:, cols])
        halo_ref[:, cols] = u[tm - SUBLANES:, :]
        p1 = prev[SUBLANES - 1:SUBLANES, :]
        p2 = prev[SUBLANES - 2:SUBLANES - 1, :]
        u1 = jnp.where(row == 0, p1, pltpu.roll(u, 1, 0))
        u2 = jnp.where(row == 0, p2, jnp.where(row == 1, p1, pltpu.roll(u, 2, 0)))
        cw = cw_ref[:, cols]
        return cw[0:1, :] * u2 + cw[1:2, :] * u1 + cw[2:3, :] * u + cb_ref[:, cols]

    for c in range(D_FF // FF_CHUNK):
        gate = conv_cols(c * FF_CHUNK)
        up = conv_cols(D_FF + c * FF_CHUNK)
        act_ref[:, c * FF_CHUNK:(c + 1) * FF_CHUNK] = (gate * jax.nn.sigmoid(gate) * up).astype(BF16)
    o_ref[...] = x + _dot(act_ref[...], wdn_ref[...])


def conv_glu_ffn(h, g, w_up, conv_w, conv_b, w_down, seq, tm=512):
    m, d = h.shape
    f2 = w_up.shape[1]
    assert seq % tm == 0 and D_FF % FF_CHUNK == 0
    const = lambda i: (0, 0)
    return pl.pallas_call(
        functools.partial(_ffn_kernel, tiles_per_seq=seq // tm),
        grid=(m // tm,),
        in_specs=[pl.BlockSpec((tm, d), lambda i: (i, 0)),
                  pl.BlockSpec((1, d), const),
                  pl.BlockSpec((d, f2), const, pipeline_mode=pl.Buffered(1)),
                  pl.BlockSpec((CONV_W, f2), const),
                  pl.BlockSpec((1, f2), const),
                  pl.BlockSpec((D_FF, d), const, pipeline_mode=pl.Buffered(1))],
        out_specs=pl.BlockSpec((tm, d), lambda i: (i, 0)),
        out_shape=jax.ShapeDtypeStruct((m, d), F32),
        scratch_shapes=[pltpu.VMEM((tm, d), BF16),
                        pltpu.VMEM((tm, D_FF), BF16),
                        pltpu.VMEM((SUBLANES, f2), F32)],
        compiler_params=_cparams(("arbitrary",)),
        name="conv_glu_ffn",
    )(h, g.reshape(1, d), w_up, conv_w, conv_b.reshape(1, f2), w_down)


def _ple_kernel(h_ref, g_ref, wg_ref, p_ref, wp_ref, o_ref):
    x = h_ref[...]
    hn = _rms(x, g_ref[...]).astype(BF16)
    gate = jax.nn.sigmoid(_dot(hn, wg_ref[...]))
    o_ref[...] = x + gate * _dot(p_ref[...].astype(BF16), wp_ref[...])


def ple(h, g, w_gate, p, w_proj, tm=512):
    m, d = h.shape
    pd = p.shape[1]
    const = lambda i: (0, 0)
    return pl.pallas_call(
        _ple_kernel,
        grid=(m // tm,),
        in_specs=[pl.BlockSpec((tm, d), lambda i: (i, 0)),
                  pl.BlockSpec((1, d), const),
                  pl.BlockSpec((d, d), const),
                  pl.BlockSpec((tm, pd), lambda i: (i, 0)),
                  pl.BlockSpec((pd, d), const)],
        out_specs=pl.BlockSpec((tm, d), lambda i: (i, 0)),
        out_shape=jax.ShapeDtypeStruct((m, d), F32),
        compiler_params=_cparams(("parallel",)),
        name="ple",
    )(h, g.reshape(1, d), w_gate, p, w_proj)


def _sb_tri():
    j = np.arange(2 * BLK)[:, None] % BLK
    s = np.arange(2 * BLK)[None, :]
    return ((s >= BLK) | (j >= s)).astype(np.float32)


def _sb_kernel(q_ref, k_ref, v_ref, tri_ref, o_ref):
    n = pl.program_id(1)
    lane = lax.broadcasted_iota(jnp.int32, (BLK, BLK), 1)
    rowi = lax.broadcasted_iota(jnp.int32, (BLK, BLK), 0)
    first_half = lane < SB_DIM
    causal = lane < rowi
    tri = tri_ref[...]
    scale = SB_DIM ** -0.5

    for p in range(SB_WIDTH // LANES):
        cols = slice(p * LANES, (p + 1) * LANES)
        q2 = q_ref[0, :, cols]
        zero = jnp.zeros_like(q2)
        q_heads = (jnp.where(first_half, q2, zero), jnp.where(first_half, zero, q2))

        def block(kb, carry, diagonal, cols=cols, q_heads=q_heads):
            acc, c0, c1 = carry
            k2 = k_ref[0, pl.ds(pl.multiple_of(kb * BLK, BLK), BLK), cols]
            v2 = v_ref[0, pl.ds(pl.multiple_of(kb * BLK, BLK), BLK), cols]
            pvs, cs = [], []
            for qh, c in zip(q_heads, (c0, c1)):
                z = _dot_nt(qh, k2) * scale
                log_keep = -(jnp.maximum(z, 0.0) + jnp.log1p(jnp.exp(-jnp.abs(z))))
                if diagonal:
                    log_keep = jnp.where(causal, log_keep, 0.0)
                hi, lo = _split_bf16(log_keep)
                sums = _dot(jnp.concatenate([hi, lo], axis=1), tri)
                incl, total = sums[:, :BLK], sums[:, BLK:]
                later = incl - log_keep + c
                w = jnp.exp(z + log_keep + later)
                if diagonal:
                    w = jnp.where(causal, w, 0.0)
                pvs.append(_dot(w.astype(BF16), v2))
                cs.append(c + total)
            return acc + jnp.where(first_half, pvs[0], pvs[1]), cs[0], cs[1]

        zf = jnp.zeros((BLK, BLK), F32)
        carry = block(n, (zf, zf, zf), True)
        carry = lax.fori_loop(0, n, lambda j, c: block(n - 1 - j, c, False), carry)
        o_ref[0, :, cols] = carry[0].astype(o_ref.dtype)


def stick_breaking(proj):
    b, s, _ = proj.shape
    tri = jnp.asarray(_sb_tri(), BF16)
    return pl.pallas_call(
        _sb_kernel,
        grid=(b, s // BLK),
        in_specs=[pl.BlockSpec((1, BLK, SB_WIDTH), lambda i, n: (i, n, 0)),
                  pl.BlockSpec((1, s, SB_WIDTH), lambda i, n: (i, 0, 1)),
                  pl.BlockSpec((1, s, SB_WIDTH), lambda i, n: (i, 0, 2)),
                  pl.BlockSpec((2 * BLK, 2 * BLK), lambda i, n: (0, 0))],
        out_specs=pl.BlockSpec((1, BLK, SB_WIDTH), lambda i, n: (i, n, 0)),
        out_shape=jax.ShapeDtypeStruct((b, s, SB_WIDTH), BF16),
        compiler_params=_cparams(("parallel", "arbitrary")),
        name="stick_breaking",
    )(proj, proj, proj, tri)


def _hg_time_mats():
    t = np.arange(BLK)[:, None]
    j = np.arange(BLK)[None, :]
    same = (t // HG_CHUNK) == (j // HG_CHUNK)
    return np.concatenate([same & (j <= t), same], axis=0).astype(np.float32)


def _hgrn_kernel(q_ref, f_ref, i_ref, g_ref, lb_ref, gn_ref, tmat_ref, o_ref,
                 qf_s, kk_s, b_s, qt_s, kt_s, dec_s):
    s_len = q_ref.shape[1]
    lb = lb_ref[0]
    tmat = tmat_ref[...]

    def prep(r, _):
        rows = pl.ds(pl.multiple_of(r * BLK, BLK), BLK)
        fp = f_ref[0, rows, :]
        q = q_ref[0, rows, :]
        log_f = jnp.log(lb + (1.0 - lb) * jax.nn.sigmoid(fp))
        kk = (1.0 - lb) * jax.nn.sigmoid(-fp)
        qf = q * jax.nn.sigmoid(q)
        hi, lo = _split_bf16(log_f)
        sums = _dot(tmat, jnp.concatenate([hi, lo], axis=1))
        b = sums[:BLK, :LANES] + sums[:BLK, LANES:]
        b_tot = sums[BLK:, :LANES] + sums[BLK:, LANES:]
        qf_s[rows, :] = qf
        kk_s[rows, :] = kk
        b_s[rows, :] = b
        qt_s[rows, :] = (qf * jnp.exp(b)).astype(BF16)
        kt_s[rows, :] = (kk * jnp.exp(b_tot - b)).astype(BF16)
        dec_s[rows, :] = jnp.exp(b_tot)
        return 0

    lax.fori_loop(0, s_len // BLK, prep, 0)

    rowi = lax.broadcasted_iota(jnp.int32, (HG_CHUNK, 1), 0)

    def chunk(c, state):
        r0 = pl.multiple_of(c * HG_CHUNK, HG_CHUNK)
        rows = pl.ds(r0, HG_CHUNK)
        o = _dot_nt(qt_s[rows, :], state.astype(BF16))
        qf = qf_s[rows, :]
        b = b_s[rows, :]
        for s in range(HG_CHUNK):
            one = pl.ds(r0 + s, 1)
            e = jnp.exp(jnp.minimum(b - b_s[one, :], 0.0))
            score = jnp.sum(qf * kk_s[one, :] * e, axis=-1, keepdims=True)
            score = jnp.where(rowi >= s, score, 0.0)
            o = o + score * i_ref[0, one, :]
        o_ref[0, rows, :] = o
        upd = _dot_tn(i_ref[0, rows, :].astype(BF16), kt_s[rows, :])
        return state * dec_s[pl.ds(r0, 1), :] + upd

    lax.fori_loop(0, s_len // HG_CHUNK, chunk, jnp.zeros((HG_DV, HG_DK), F32))

    def finish(r, _):
        rows = pl.ds(pl.multiple_of(r * BLK, BLK), BLK)
        g = g_ref[0, rows, :]
        o_ref[0, rows, :] = _rms(o_ref[0, rows, :], gn_ref[...]) * (g * jax.nn.sigmoid(g))
        return 0

    lax.fori_loop(0, s_len // BLK, finish, 0)


def hgrn2(proj, lb, out_norm):
    b, s, _ = proj.shape
    tmat = jnp.asarray(_hg_time_mats(), BF16)
    col = lambda part: (lambda i, h: (i, 0, part * HG_HEADS + h))
    seq_blk = lambda part: pl.BlockSpec((1, s, LANES), col(part))
    return pl.pallas_call(
        _hgrn_kernel,
        grid=(b, HG_HEADS),
        in_specs=[seq_blk(0), seq_blk(1), seq_blk(2), seq_blk(3),
                  pl.BlockSpec((1, 1, HG_DK), lambda i, h: (h, 0, 0)),
                  pl.BlockSpec((1, HG_DV), lambda i, h: (0, 0)),
                  pl.BlockSpec((2 * BLK, BLK), lambda i, h: (0, 0))],
        out_specs=pl.BlockSpec((1, s, LANES), lambda i, h: (i, 0, h)),
        out_shape=jax.ShapeDtypeStruct((b, s, HG_V), F32),
        scratch_shapes=[pltpu.VMEM((s, HG_DK), F32), pltpu.VMEM((s, HG_DK), F32),
                        pltpu.VMEM((s, HG_DK), F32), pltpu.VMEM((s, HG_DK), BF16),
                        pltpu.VMEM((s, HG_DK), BF16), pltpu.VMEM((s, HG_DK), F32)],
        compiler_params=_cparams(("parallel", "parallel")),
        name="hgrn2",
    )(proj, proj, proj, proj, lb.reshape(HG_HEADS, 1, HG_DK), out_norm.reshape(1, HG_DV), tmat)


def _t5_band_buckets():
    t = np.arange(WINDOW)[:, None]
    s = np.arange(2 * WINDOW)[None, :]
    dist = t + WINDOW - s
    max_exact = N_BUCKETS // 2
    large = max_exact + (np.log(np.maximum(dist, max_exact) / max_exact) / math.log(MAX_DISTANCE / max_exact)
                         * (N_BUCKETS - max_exact)).astype(np.int32)
    large = np.minimum(large, N_BUCKETS - 1)
    return np.where(dist < max_exact, np.maximum(dist, 0), large).astype(np.int32)


def _swa_kernel(sink_ref, q_ref, kc_ref, kp_ref, vc_ref, vp_ref, bias_ref, qn_ref, kn_ref, o_ref):
    n = pl.program_id(1)
    col = lax.broadcasted_iota(jnp.int32, (WINDOW, 2 * WINDOW), 1)
    row = lax.broadcasted_iota(jnp.int32, (WINDOW, 2 * WINDOW), 0)
    dist = row + WINDOW - col
    mask = (dist >= 0) & (dist < WINDOW) & ((col >= WINDOW) | (n > 0))
    scale = SW_DIM ** -0.5
    for g in range(SW_KV_HEADS):
        kcols = slice(g * SW_DIM, (g + 1) * SW_DIM)
        k = jnp.concatenate([kp_ref[0, :, kcols], kc_ref[0, :, kcols]], axis=0)
        k = _rms(k, kn_ref[...]).astype(BF16)
        v = jnp.concatenate([vp_ref[0, :, kcols], vc_ref[0, :, kcols]], axis=0).astype(BF16)
        for hh in range(SW_GROUP):
            h = g * SW_GROUP + hh
            hcols = slice(h * SW_DIM, (h + 1) * SW_DIM)
            q = (_rms(q_ref[0, :, hcols], qn_ref[...]) * scale).astype(BF16)
            logits = _dot_nt(q, k) + bias_ref[h]
            logits = jnp.where(mask, logits, -jnp.inf)
            sink = sink_ref[h]
            m = jnp.maximum(jnp.max(logits, axis=-1, keepdims=True), sink)
            e = jnp.exp(logits - m)
            den = jnp.sum(e, axis=-1, keepdims=True) + jnp.exp(sink - m)
            w = e * (1.0 / den)
            o_ref[0, :, hcols] = _dot(w.astype(BF16), v).astype(o_ref.dtype)


def sliding_window(proj, q_norm, k_norm, sinks, rel_bias):
    b, s, _ = proj.shape
    bias = rel_bias.astype(F32)[_t5_band_buckets()].transpose(2, 0, 1)
    kv_w = SW_KV_HEADS * SW_DIM
    q_w = SW_HEADS * SW_DIM
    k_blk = q_w // kv_w
    prev = lambda n: jnp.maximum(n - 1, 0)
    return pl.pallas_call(
        _swa_kernel,
        grid=(b, s // WINDOW),
        in_specs=[pl.BlockSpec(memory_space=pltpu.SMEM),
                  pl.BlockSpec((1, WINDOW, q_w), lambda i, n: (i, n, 0)),
                  pl.BlockSpec((1, WINDOW, kv_w), lambda i, n: (i, n, k_blk)),
                  pl.BlockSpec((1, WINDOW, kv_w), lambda i, n: (i, prev(n), k_blk)),
                  pl.BlockSpec((1, WINDOW, kv_w), lambda i, n: (i, n, k_blk + 1)),
                  pl.BlockSpec((1, WINDOW, kv_w), lambda i, n: (i, prev(n), k_blk + 1)),
                  pl.BlockSpec((SW_HEADS, WINDOW, 2 * WINDOW), lambda i, n: (0, 0, 0)),
                  pl.BlockSpec((1, SW_DIM), lambda i, n: (0, 0)),
                  pl.BlockSpec((1, SW_DIM), lambda i, n: (0, 0))],
        out_specs=pl.BlockSpec((1, WINDOW, q_w), lambda i, n: (i, n, 0)),
        out_shape=jax.ShapeDtypeStruct((b, s, q_w), BF16),
        compiler_params=_cparams(("parallel", "parallel")),
        name="sliding_window",
    )(sinks.astype(F32), proj, proj, proj, proj, proj, bias,
      q_norm.reshape(1, SW_DIM), k_norm.reshape(1, SW_DIM))


def kernel(x, p, mix_norm, ab_w_in, hg_lb_logits, hg_out_norm, ab_w_out, c_w_in, q_norm, k_norm, sinks,
           rel_bias, c_w_out, ffn_norm, ffn_up, ffn_conv, ffn_conv_b, ffn_down, ple_norm, ple_gate, ple_proj):
    b, s, d = x.shape
    depth = p.shape[0]
    m = b * s
    lb_cum = jnp.cumsum(jax.nn.softmax(hg_lb_logits.astype(F32), axis=0), axis=0)
    lower_bounds = lb_cum - lb_cum[0]
    h = x.reshape(m, d)
    for i in range(depth):
        j = i // 2
        if i % 2 == 0:
            att, gates = norm_proj(h, mix_norm[i], ab_w_in[j].astype(BF16),
                                   (3 * SB_WIDTH, 2 * HG_QK + 2 * HG_V), (BF16, F32))
            o_a = stick_breaking(att.reshape(b, s, -1))
            o_b = hgrn2(gates.reshape(b, s, -1), lower_bounds[j], hg_out_norm[j])
            mixed = jnp.concatenate([o_a.reshape(m, -1), o_b.astype(BF16).reshape(m, -1)], axis=-1)
            h = proj_residual(mixed, ab_w_out[j].astype(BF16), h)
        else:
            (qkv,) = norm_proj(h, mix_norm[i], c_w_in[j].astype(BF16), (c_w_in.shape[2],), (F32,))
            o = sliding_window(qkv.reshape(b, s, -1), q_norm[j], k_norm[j], sinks[j], rel_bias)
            h = proj_residual(o.reshape(m, -1), c_w_out[j].astype(BF16), h)
        h = conv_glu_ffn(h, ffn_norm[i], ffn_up[i].astype(BF16), ffn_conv[i], ffn_conv_b[i],
                         ffn_down[i].astype(BF16), s)
        h = ple(h, ple_norm[i], ple_gate[i].astype(BF16), p[i].reshape(m, -1), ple_proj[i].astype(BF16))
    return h.reshape(b, s, d)
```

```python
import functools
import math

import numpy as np
import jax
import jax.numpy as jnp
from jax import lax
from jax.experimental import pallas as pl
from jax.experimental.pallas import tpu as pltpu

F32 = jnp.float32
BF16 = jnp.bfloat16

D_MODEL = 1024
PLE_DIM = 256
EPS = 1e-6
SB_HEADS = 8
SB_DIM = 64
SB_WIDTH = SB_HEADS * SB_DIM
HG_HEADS = 4
HG_DK = 128
HG_DV = 128
HG_QK = HG_HEADS * HG_DK
HG_V = HG_HEADS * HG_DV
SW_HEADS = 16
SW_KV_HEADS = 4
SW_DIM = 64
SW_GROUP = SW_HEADS // SW_KV_HEADS
WINDOW = 128
N_BUCKETS = 32
MAX_DISTANCE = 128
D_FF = 2816
CONV_W = 3

LANES = 128
SUBLANES = 8
BLK = 128
HG_CHUNK = 16
FF_CHUNK = 256
SB_DEAD_LOG2 = -150.0
SB_NEAR_ROWS = 32
LOG2E = 1.4426950408889634
VMEM_LIMIT = 56 * 1024 * 1024


def _cparams(sem):
    return pltpu.CompilerParams(dimension_semantics=sem, vmem_limit_bytes=VMEM_LIMIT)


def _rms(x, g):
    ms = jnp.mean(x * x, axis=-1, keepdims=True)
    return x * lax.rsqrt(ms + EPS) * g


def _dot(a, b):
    return jnp.dot(a, b, preferred_element_type=F32)


def _dot_nt(a, b):
    return lax.dot_general(a, b, (((1,), (1,)), ((), ())), preferred_element_type=F32)


def _dot_tn(a, b):
    return lax.dot_general(a, b, (((0,), (0,)), ((), ())), preferred_element_type=F32)


def _split_bf16(x):
    hi = x.astype(BF16)
    lo = (x - hi.astype(F32)).astype(BF16)
    return hi, lo


def _norm_proj_kernel(h_ref, g_ref, w_ref, *o_refs, splits, nchunk):
    hn = _rms(h_ref[...], g_ref[...]).astype(BF16)
    col = 0
    for o_ref, width in zip(o_refs, splits):
        for c0 in range(0, width, nchunk):
            cw = min(nchunk, width - c0)
            o_ref[:, c0:c0 + cw] = _dot(hn, w_ref[:, col + c0:col + c0 + cw]).astype(o_ref.dtype)
        col += width


def norm_proj(h, g, w, splits, dtypes, tm=512, nchunk=512):
    m, k = h.shape
    n = w.shape[1]
    assert sum(splits) == n and m % tm == 0
    return pl.pallas_call(
        functools.partial(_norm_proj_kernel, splits=tuple(splits), nchunk=nchunk),
        grid=(m // tm,),
        in_specs=[pl.BlockSpec((tm, k), lambda i: (i, 0)),
                  pl.BlockSpec((1, k), lambda i: (0, 0)),
                  pl.BlockSpec((k, n), lambda i: (0, 0))],
        out_specs=[pl.BlockSpec((tm, s), lambda i: (i, 0)) for s in splits],
        out_shape=[jax.ShapeDtypeStruct((m, s), d) for s, d in zip(splits, dtypes)],
        compiler_params=_cparams(("parallel",)),
        name="norm_proj",
    )(h, g.reshape(1, k), w)


def _proj_res_kernel(a_ref, w_ref, h_ref, o_ref):
    o_ref[...] = h_ref[...] + _dot(a_ref[...].astype(BF16), w_ref[...])


def proj_residual(a, w, h, tm=512):
    m, k = a.shape
    n = w.shape[1]
    return pl.pallas_call(
        _proj_res_kernel,
        grid=(m // tm,),
        in_specs=[pl.BlockSpec((tm, k), lambda i: (i, 0)),
                  pl.BlockSpec((k, n), lambda i: (0, 0)),
                  pl.BlockSpec((tm, n), lambda i: (i, 0))],
        out_specs=pl.BlockSpec((tm, n), lambda i: (i, 0)),
        out_shape=jax.ShapeDtypeStruct((m, n), F32),
        compiler_params=_cparams(("parallel",)),
        name="proj_residual",
    )(a, w, h)


def _ffn_kernel(h_ref, g_ref, wup_ref, cw_ref, cb_ref, wdn_ref, o_ref, hn_ref, act_ref, halo_ref,
                *, tiles_per_seq):
    tm = h_ref.shape[0]
    x = h_ref[...]
    hn_ref[...] = _rms(x, g_ref[...]).astype(BF16)
    seq_start = (pl.program_id(0) % tiles_per_seq) == 0
    row = lax.broadcasted_iota(jnp.int32, (tm, FF_CHUNK), 0)

    def conv_cols(c0):
        cols = slice(c0, c0 + FF_CHUNK)
        u = _dot(hn_ref[...], wup_ref[:, cols])
        prev = jnp.where(seq_start, 0.0, halo_ref[:, cols])
        halo_ref[:, cols] = u[tm - SUBLANES:, :]
        p1 = prev[SUBLANES - 1:SUBLANES, :]
        p2 = prev[SUBLANES - 2:SUBLANES - 1, :]
        u1 = jnp.where(row == 0, p1, pltpu.roll(u, 1, 0))
        u2 = jnp.where(row == 0, p2, jnp.where(row == 1, p1, pltpu.roll(u, 2, 0)))
        cw = cw_ref[:, cols]
        return cw[0:1, :] * u2 + cw[1:2, :] * u1 + cw[2:3, :] * u + cb_ref[:, cols]

    for c in range(D_FF // FF_CHUNK):
        gate = conv_cols(c * FF_CHUNK)
        up = conv_cols(D_FF + c * FF_CHUNK)
        act_ref[:, c * FF_CHUNK:(c + 1) * FF_CHUNK] = (gate * jax.nn.sigmoid(gate) * up).astype(BF16)
    o_ref[...] = x + _dot(act_ref[...], wdn_ref[...])


def conv_glu_ffn(h, g, w_up, conv_w, conv_b, w_down, seq, tm=512):
    m, d = h.shape
    f2 = w_up.shape[1]
    assert seq % tm == 0 and D_FF % FF_CHUNK == 0
    const = lambda i: (0, 0)
    return pl.pallas_call(
        functools.partial(_ffn_kernel, tiles_per_seq=seq // tm),
        grid=(m // tm,),
        in_specs=[pl.BlockSpec((tm, d), lambda i: (i, 0)),
                  pl.BlockSpec((1, d), const),
                  pl.BlockSpec((d, f2), const, pipeline_mode=pl.Buffered(1)),
                  pl.BlockSpec((CONV_W, f2), const),
                  pl.BlockSpec((1, f2), const),
                  pl.BlockSpec((D_FF, d), const, pipeline_mode=pl.Buffered(1))],
        out_specs=pl.BlockSpec((tm, d), lambda i: (i, 0)),
        out_shape=jax.ShapeDtypeStruct((m, d), F32),
        scratch_shapes=[pltpu.VMEM((tm, d), BF16),
                        pltpu.VMEM((tm, D_FF), BF16),
                        pltpu.VMEM((SUBLANES, f2), F32)],
        compiler_params=_cparams(("arbitrary",)),
        name="conv_glu_ffn",
    )(h, g.reshape(1, d), w_up, conv_w, conv_b.reshape(1, f2), w_down)


def _ple_kernel(h_ref, g_ref, wg_ref, p_ref, wp_ref, o_ref):
    x = h_ref[...]
    hn = _rms(x, g_ref[...]).astype(BF16)
    gate = jax.nn.sigmoid(_dot(hn, wg_ref[...]))
    o_ref[...] = x + gate * _dot(p_ref[...].astype(BF16), wp_ref[...])


def ple(h, g, w_gate, p, w_proj, tm=512):
    m, d = h.shape
    pd = p.shape[1]
    const = lambda i: (0, 0)
    return pl.pallas_call(
        _ple_kernel,
        grid=(m // tm,),
        in_specs=[pl.BlockSpec((tm, d), lambda i: (i, 0)),
                  pl.BlockSpec((1, d), const),
                  pl.BlockSpec((d, d), const),
                  pl.BlockSpec((tm, pd), lambda i: (i, 0)),
                  pl.BlockSpec((pd, d), const)],
        out_specs=pl.BlockSpec((tm, d), lambda i: (i, 0)),
        out_shape=jax.ShapeDtypeStruct((m, d), F32),
        compiler_params=_cparams(("parallel",)),
        name="ple",
    )(h, g.reshape(1, d), w_gate, p, w_proj)


def _sb_tri():
    j = np.arange(2 * BLK)[:, None] % BLK
    s = np.arange(2 * BLK)[None, :]
    return ((s >= BLK) | (j >= s)).astype(np.float32)


def _neg_abs(x):
    sign = jnp.uint32(0x80000000)
    return lax.bitcast_convert_type(lax.bitcast_convert_type(x, jnp.uint32) | sign, F32)


def _sb_kernel(q_ref, k_ref, v_ref, tri_ref, o_ref, qs_ref, acc_ref, c_ref):
    n = pl.program_id(1)
    npair = SB_WIDTH // LANES
    pairs = range(npair)
    cols = [slice(p * LANES, (p + 1) * LANES) for p in pairs]
    lane = lax.broadcasted_iota(jnp.int32, (2 * BLK, BLK), 1)
    row = lax.broadcasted_iota(jnp.int32, (2 * BLK, BLK), 0)
    first_head = row < BLK
    causal = lane < jnp.where(first_head, row, row - BLK)
    own_lanes = first_head == (lane < SB_DIM)

    for p in pairs:
        q2 = q_ref[0, :, cols[p]] * (SB_DIM ** -0.5)
        q2 = jnp.concatenate([q2, q2], axis=0)
        qs_ref[p] = jnp.where(own_lanes, q2, jnp.zeros_like(q2))

    def stacked(ref, p, r):
        if r == BLK:
            return ref[p]
        return jnp.concatenate([ref[p, 0:r], ref[p, BLK:BLK + r]], axis=0)

    def process(kb, nblk, diagonal, r):
        krows = pl.ds(pl.multiple_of(kb * BLK, BLK), nblk * BLK)
        first_half = lax.broadcasted_iota(jnp.int32, (r, BLK), 1) < SB_DIM
        qs = [stacked(qs_ref, p, r) for p in pairs]
        tn = [_dot_nt(qs[p], k_ref[0, krows, cols[p]]) * (-LOG2E) for p in pairs]
        order = list(range(nblk - 1, -1, -1))
        split = {}
        for p in pairs:
            for j in order:
                t = tn[p][:, j * BLK:(j + 1) * BLK]
                log_keep = jnp.minimum(t, 0.0) - jnp.log2(1.0 + jnp.exp2(_neg_abs(t)))
                if diagonal and j == nblk - 1:
                    log_keep = jnp.where(causal, log_keep, 0.0)
                split[p, j] = jnp.concatenate(_split_bf16(log_keep), axis=1)
        sums = {(p, j): _dot(split[p, j], tri_ref[...]) for p in pairs for j in order}
        ws = []
        for p in pairs:
            c = None if diagonal else stacked(c_ref, p, r)
            w = [None] * nblk
            for j in order:
                incl, total = sums[p, j][:, :BLK], sums[p, j][:, BLK:]
                arg = incl - tn[p][:, j * BLK:(j + 1) * BLK]
                if c is not None:
                    arg = arg + c
                wj = jnp.exp2(arg)
                if diagonal and j == nblk - 1:
                    wj = jnp.where(causal, wj, 0.0)
                w[j] = wj.astype(BF16)
                c = total if c is None else c + total
            ws.append(w[0] if nblk == 1 else jnp.concatenate(w, axis=1))
            if r == BLK:
                c_ref[p] = c
            else:
                c_ref[p, 0:r] = c[:r]
                c_ref[p, BLK:BLK + r] = c[r:]
        pvs = [_dot(ws[p], v_ref[0, krows, cols[p]]) for p in pairs]
        for p in pairs:
            pv = jnp.where(first_half, pvs[p][:r], pvs[p][r:])
            if diagonal:
                acc_ref[p] = pv
            else:
                acc_ref[p, 0:r] += pv

    @pl.when(n == 0)
    def _():
        process(n, 1, True, BLK)

    @pl.when(n > 0)
    def _():
        process(n - 1, 2, True, BLK)

    def live(lo, hi):
        c = c_ref[...]
        return jnp.max(jnp.maximum(c[:, lo:hi], c[:, BLK + lo:BLK + hi])) > SB_DEAD_LOG2

    def full_block(state):
        process(state[0], 1, False, BLK)
        return state[0] - 1, live(SB_NEAR_ROWS, BLK)

    def near_block(state):
        process(state[0], 1, False, SB_NEAR_ROWS)
        return state[0] - 1, live(0, SB_NEAR_ROWS)

    more = lambda st: (st[0] >= 0) & st[1]
    kb, _ = lax.while_loop(more, full_block, (n - 2, live(SB_NEAR_ROWS, BLK)))
    lax.while_loop(more, near_block, (kb, live(0, SB_NEAR_ROWS)))
    for p in pairs:
        o_ref[0, :, cols[p]] = acc_ref[p].astype(o_ref.dtype)


def stick_breaking(proj):
    b, s, _ = proj.shape
    tri = jnp.asarray(_sb_tri(), BF16)
    npair = SB_WIDTH // LANES
    return pl.pallas_call(
        _sb_kernel,
        grid=(b, s // BLK),
        in_specs=[pl.BlockSpec((1, BLK, SB_WIDTH), lambda i, n: (i, n, 0)),
                  pl.BlockSpec((1, s, SB_WIDTH), lambda i, n: (i, 0, 1)),
                  pl.BlockSpec((1, s, SB_WIDTH), lambda i, n: (i, 0, 2)),
                  pl.BlockSpec((2 * BLK, 2 * BLK), lambda i, n: (0, 0))],
        out_specs=pl.BlockSpec((1, BLK, SB_WIDTH), lambda i, n: (i, n, 0)),
        out_shape=jax.ShapeDtypeStruct((b, s, SB_WIDTH), BF16),
        scratch_shapes=[pltpu.VMEM((npair, 2 * BLK, LANES), BF16),
                        pltpu.VMEM((npair, BLK, LANES), F32),
                        pltpu.VMEM((npair, 2 * BLK, LANES), F32)],
        compiler_params=_cparams(("parallel", "arbitrary")),
        name="stick_breaking",
    )(proj, proj, proj, tri)


def _hg_time_mats():
    t = np.arange(BLK)[:, None]
    j = np.arange(BLK)[None, :]
    same = (t // HG_CHUNK) == (j // HG_CHUNK)
    per_chunk = np.arange(2 * SUBLANES)[:, None] == (j // HG_CHUNK)
    return np.concatenate([same & (j <= t), same, per_chunk], axis=0).astype(np.float32)


def _hgrn_kernel(q_ref, f_ref, i_ref, g_ref, lb_ref, gn_ref, tmat_ref, o_ref,
                 qf_s, kk_s, b_s, iv_s, qt_s, kt_s, dec_s, state_s):
    ts = q_ref.shape[1]
    heads = range(HG_HEADS)
    hcols = [slice(h * LANES, (h + 1) * LANES) for h in heads]
    half = HG_CHUNK // 2
    cpb = BLK // HG_CHUNK

    @pl.when(pl.program_id(1) == 0)
    def _():
        state_s[...] = jnp.zeros_like(state_s)

    def prep(r, _):
        rows = pl.ds(pl.multiple_of(r * BLK, BLK), BLK)
        for h in heads:
            lb = lb_ref[:, hcols[h]]
            fp = f_ref[0, rows, hcols[h]]
            q = q_ref[0, rows, hcols[h]]
            e = jnp.exp(-jnp.abs(fp))
            sg = 1.0 / (1.0 + e)
            log_f = jnp.log(lb + (1.0 - lb) * jnp.where(fp >= 0, sg, e * sg))
            kk = (1.0 - lb) * jnp.where(fp >= 0, e * sg, sg)
            qf = q * jax.nn.sigmoid(q)
            hi, lo = _split_bf16(log_f)
            sums = _dot(tmat_ref[...], jnp.concatenate([hi, lo], axis=1))
            sums = sums[:, :LANES] + sums[:, LANES:]
            b = sums[:BLK]
            b_tot = sums[BLK:2 * BLK]
            chunk_tot = sums[2 * BLK:2 * BLK + cpb]
            qf_s[h, rows, :] = qf
            kk_s[h, rows, :] = kk
            b_s[h, rows, :] = b
            iv_s[h, rows, :] = i_ref[0, rows, hcols[h]]
            qt_s[h, rows, :] = (qf * jnp.exp(b)).astype(BF16)
            kt_s[h, rows, :] = (kk * jnp.exp(b_tot - b)).astype(BF16)
            dec_s[h, pl.ds(pl.multiple_of(r * cpb, cpb), cpb), :] = jnp.exp(chunk_tot)
        return 0

    lax.fori_loop(0, ts // BLK, prep, 0)

    rowi = lax.broadcasted_iota(jnp.int32, (half, 1), 0)

    def chunk(c, _):
        r0 = pl.multiple_of(c * HG_CHUNK, HG_CHUNK)
        rows = pl.ds(r0, HG_CHUNK)
        lo_rows = pl.ds(r0, half)
        hi_rows = pl.ds(r0 + half, half)
        inter = [_dot_nt(qt_s[h, rows, :], state_s[h].astype(BF16)) for h in heads]
        for h in heads:
            cols = hcols[h]
            qf_lo, qf_hi = qf_s[h, lo_rows, :], qf_s[h, hi_rows, :]
            b_lo, b_hi = b_s[h, lo_rows, :], b_s[h, hi_rows, :]
            o_lo = o_hi = jnp.zeros((half, LANES), F32)
            for s in range(HG_CHUNK):
                one = pl.ds(r0 + s, 1)
                b_key, k_key, v_key = b_s[h, one, :], kk_s[h, one, :], iv_s[h, one, :]
                if s < half:
                    score = jnp.sum(qf_lo * k_key * jnp.exp(b_lo - b_key), axis=-1, keepdims=True)
                    if s > 0:
                        score = jnp.where(rowi >= s, score, 0.0)
                    o_lo = o_lo + score * v_key
                score = jnp.sum(qf_hi * k_key * jnp.exp(b_hi - b_key), axis=-1, keepdims=True)
                if s > half:
                    score = jnp.where(rowi >= s - half, score, 0.0)
                o_hi = o_hi + score * v_key
            o_ref[0, lo_rows, cols] = o_lo + inter[h][:half]
            o_ref[0, hi_rows, cols] = o_hi + inter[h][half:]
        upd = [_dot_tn(iv_s[h, rows, :].astype(BF16), kt_s[h, rows, :]) for h in heads]
        for h in heads:
            state_s[h] = state_s[h] * dec_s[h, pl.ds(c, 1), :] + upd[h]
        return 0

    lax.fori_loop(0, ts // HG_CHUNK, chunk, 0)

    def finish(r, _):
        rows = pl.ds(pl.multiple_of(r * BLK, BLK), BLK)
        for h in heads:
            g = g_ref[0, rows, hcols[h]]
            o_ref[0, rows, hcols[h]] = _rms(o_ref[0, rows, hcols[h]], gn_ref[...]) * (g * jax.nn.sigmoid(g))
        return 0

    lax.fori_loop(0, ts // BLK, finish, 0)


def hgrn2(proj, lb, out_norm, ts=512):
    b, s, _ = proj.shape
    assert s % ts == 0 and ts % BLK == 0
    tmat = jnp.asarray(_hg_time_mats(), BF16)
    seq_blk = lambda part: pl.BlockSpec((1, ts, HG_QK), lambda i, t: (i, t, part))
    const = lambda i, t: (0, 0)
    return pl.pallas_call(
        _hgrn_kernel,
        grid=(b, s // ts),
        in_specs=[seq_blk(0), seq_blk(1), seq_blk(2), seq_blk(3),
                  pl.BlockSpec((1, HG_QK), const),
                  pl.BlockSpec((1, HG_DV), const),
                  pl.BlockSpec(tmat.shape, const)],
        out_specs=pl.BlockSpec((1, ts, HG_V), lambda i, t: (i, t, 0)),
        out_shape=jax.ShapeDtypeStruct((b, s, HG_V), F32),
        scratch_shapes=[pltpu.VMEM((HG_HEADS, ts, HG_DK), F32), pltpu.VMEM((HG_HEADS, ts, HG_DK), F32),
                        pltpu.VMEM((HG_HEADS, ts, HG_DK), F32), pltpu.VMEM((HG_HEADS, ts, HG_DV), F32),
                        pltpu.VMEM((HG_HEADS, ts, HG_DK), BF16), pltpu.VMEM((HG_HEADS, ts, HG_DK), BF16),
                        pltpu.VMEM((HG_HEADS, ts // HG_CHUNK, HG_DK), F32),
                        pltpu.VMEM((HG_HEADS, HG_DV, HG_DK), F32)],
        compiler_params=_cparams(("parallel", "arbitrary")),
        name="hgrn2",
    )(proj, proj, proj, proj, lb.reshape(1, HG_QK), out_norm.reshape(1, HG_DV), tmat)


def _t5_band_buckets():
    t = np.arange(WINDOW)[:, None]
    s = np.arange(2 * WINDOW)[None, :]
    dist = t + WINDOW - s
    max_exact = N_BUCKETS // 2
    large = max_exact + (np.log(np.maximum(dist, max_exact) / max_exact) / math.log(MAX_DISTANCE / max_exact)
                         * (N_BUCKETS - max_exact)).astype(np.int32)
    large = np.minimum(large, N_BUCKETS - 1)
    band = (dist >= 0) & (dist < WINDOW)
    return np.where(dist < max_exact, np.maximum(dist, 0), large).astype(np.int32), band


def _swa_kernel(sink_ref, q_ref, kc_ref, kp_ref, vc_ref, vp_ref, bias_ref, qn_ref, kn_ref, o_ref):
    n = pl.program_id(1)
    groups = range(SW_KV_HEADS)
    scale = SW_DIM ** -0.5
    no_prev = jnp.where(n > 0, 0.0, -jnp.inf)
    ks, vs, qs = [], [], []
    for g in groups:
        kcols = slice(g * SW_DIM, (g + 1) * SW_DIM)
        k = jnp.concatenate([kp_ref[0, :, kcols], kc_ref[0, :, kcols]], axis=0)
        ks.append(_rms(k, kn_ref[...]).astype(BF16))
        vs.append(jnp.concatenate([vp_ref[0, :, kcols], vc_ref[0, :, kcols]], axis=0).astype(BF16))
        heads = [q_ref[0, :, (g * SW_GROUP + hh) * SW_DIM:(g * SW_GROUP + hh + 1) * SW_DIM]
                 for hh in range(SW_GROUP)]
        qs.append((_rms(jnp.concatenate(heads, axis=0), qn_ref[...]) * scale).astype(BF16))
    logits = [_dot_nt(qs[g], ks[g]) for g in groups]
    ws, invs = [], []
    for g in groups:
        w_g, inv_g = [], []
        for hh in range(SW_GROUP):
            h = g * SW_GROUP + hh
            lg = logits[g][hh * WINDOW:(hh + 1) * WINDOW]
            lp = lg[:, :WINDOW] + (bias_ref[h, :, :WINDOW] + no_prev)
            lc = lg[:, WINDOW:] + bias_ref[h, :, WINDOW:]
            sink = sink_ref[h]
            m = jnp.maximum(jnp.max(jnp.maximum(lp, lc), axis=-1, keepdims=True), sink)
            ep, ec = jnp.exp(lp - m), jnp.exp(lc - m)
            den = jnp.sum(ep + ec, axis=-1, keepdims=True) + jnp.exp(sink - m)
            w_g.append(jnp.concatenate([ep, ec], axis=1).astype(BF16))
            inv_g.append(1.0 / den)
        ws.append(jnp.concatenate(w_g, axis=0))
        invs.append(jnp.concatenate(inv_g, axis=0))
    outs = [_dot(ws[g], vs[g]) for g in groups]
    for g in groups:
        o = outs[g] * invs[g]
        for hh in range(SW_GROUP):
            h = g * SW_GROUP + hh
            o_ref[0, :, h * SW_DIM:(h + 1) * SW_DIM] = o[hh * WINDOW:(hh + 1) * WINDOW].astype(o_ref.dtype)


def sliding_window(proj, q_norm, k_norm, sinks, rel_bias):
    b, s, _ = proj.shape
    bucket, band = _t5_band_buckets()
    bias = rel_bias.astype(F32)[bucket].transpose(2, 0, 1)
    bias = jnp.where(band, bias, -jnp.inf)
    kv_w = SW_KV_HEADS * SW_DIM
    q_w = SW_HEADS * SW_DIM
    k_blk = q_w // kv_w
    prev = lambda n: jnp.maximum(n - 1, 0)
    return pl.pallas_call(
        _swa_kernel,
        grid=(b, s // WINDOW),
        in_specs=[pl.BlockSpec(memory_space=pltpu.SMEM),
                  pl.BlockSpec((1, WINDOW, q_w), lambda i, n: (i, n, 0)),
                  pl.BlockSpec((1, WINDOW, kv_w), lambda i, n: (i, n, k_blk)),
                  pl.BlockSpec((1, WINDOW, kv_w), lambda i, n: (i, prev(n), k_blk)),
                  pl.BlockSpec((1, WINDOW, kv_w), lambda i, n: (i, n, k_blk + 1)),
                  pl.BlockSpec((1, WINDOW, kv_w), lambda i, n: (i, prev(n), k_blk + 1)),
                  pl.BlockSpec((SW_HEADS, WINDOW, 2 * WINDOW), lambda i, n: (0, 0, 0)),
                  pl.BlockSpec((1, SW_DIM), lambda i, n: (0, 0)),
                  pl.BlockSpec((1, SW_DIM), lambda i, n: (0, 0))],
        out_specs=pl.BlockSpec((1, WINDOW, q_w), lambda i, n: (i, n, 0)),
        out_shape=jax.ShapeDtypeStruct((b, s, q_w), BF16),
        compiler_params=_cparams(("parallel", "parallel")),
        name="sliding_window",
    )(sinks.astype(F32), proj, proj, proj, proj, proj, bias,
      q_norm.reshape(1, SW_DIM), k_norm.reshape(1, SW_DIM))


def kernel(x, p, mix_norm, ab_w_in, hg_lb_logits, hg_out_norm, ab_w_out, c_w_in, q_norm, k_norm, sinks,
           rel_bias, c_w_out, ffn_norm, ffn_up, ffn_conv, ffn_conv_b, ffn_down, ple_norm, ple_gate, ple_proj):
    b, s, d = x.shape
    depth = p.shape[0]
    m = b * s
    lb_cum = jnp.cumsum(jax.nn.softmax(hg_lb_logits.astype(F32), axis=0), axis=0)
    lower_bounds = lb_cum - lb_cum[0]
    h = x.reshape(m, d)
    for i in range(depth):
        j = i // 2
        if i % 2 == 0:
            att, gates = norm_proj(h, mix_norm[i], ab_w_in[j].astype(BF16),
                                   (3 * SB_WIDTH, 2 * HG_QK + 2 * HG_V), (BF16, F32))
            o_a = stick_breaking(att.reshape(b, s, -1))
            o_b = hgrn2(gates.reshape(b, s, -1), lower_bounds[j], hg_out_norm[j])
            mixed = jnp.concatenate([o_a.reshape(m, -1), o_b.astype(BF16).reshape(m, -1)], axis=-1)
            h = proj_residual(mixed, ab_w_out[j].astype(BF16), h)
        else:
            (qkv,) = norm_proj(h, mix_norm[i], c_w_in[j].astype(BF16), (c_w_in.shape[2],), (F32,))
            o = sliding_window(qkv.reshape(b, s, -1), q_norm[j], k_norm[j], sinks[j], rel_bias)
            h = proj_residual(o.reshape(m, -1), c_w_out[j].astype(BF16), h)
        h = conv_glu_ffn(h, ffn_norm[i], ffn_up[i].astype(BF16), ffn_conv[i], ffn_conv_b[i],
                         ffn_down[i].astype(BF16), s)
        h = ple(h, ple_norm[i], ple_gate[i].astype(BF16), p[i].reshape(m, -1), ple_proj[i].astype(BF16))
    return h.reshape(b, s, d)
```

```python
import functools
import math

import numpy as np
import jax
import jax.numpy as jnp
from jax import lax
from jax.experimental import pallas as pl
from jax.experimental.pallas import tpu as pltpu

F32 = jnp.float32
BF16 = jnp.bfloat16

D_MODEL = 1024
PLE_DIM = 256
EPS = 1e-6
SB_HEADS = 8
SB_DIM = 64
SB_WIDTH = SB_HEADS * SB_DIM
HG_HEADS = 4
HG_DK = 128
HG_DV = 128
HG_QK = HG_HEADS * HG_DK
HG_V = HG_HEADS * HG_DV
SW_HEADS = 16
SW_KV_HEADS = 4
SW_DIM = 64
SW_GROUP = SW_HEADS // SW_KV_HEADS
WINDOW = 128
N_BUCKETS = 32
MAX_DISTANCE = 128
D_FF = 2816
CONV_W = 3

LANES = 128
SUBLANES = 8
BLK = 128
HG_CHUNK = 16
FF_CHUNK = 256
SB_DEAD_LOG2 = -150.0
SB_NEAR_ROWS = 32
LOG2E = 1.4426950408889634
VMEM_LIMIT = 56 * 1024 * 1024


def _cparams(sem):
    return pltpu.CompilerParams(dimension_semantics=sem, vmem_limit_bytes=VMEM_LIMIT)


def _rms(x, g):
    ms = jnp.mean(x * x, axis=-1, keepdims=True)
    return x * lax.rsqrt(ms + EPS) * g


def _dot(a, b):
    return jnp.dot(a, b, preferred_element_type=F32)


def _dot_nt(a, b):
    return lax.dot_general(a, b, (((1,), (1,)), ((), ())), preferred_element_type=F32)


def _dot_tn(a, b):
    return lax.dot_general(a, b, (((0,), (0,)), ((), ())), preferred_element_type=F32)


def _split_bf16(x):
    hi = x.astype(BF16)
    lo = (x - hi.astype(F32)).astype(BF16)
    return hi, lo


def _norm_proj_kernel(h_ref, g_ref, w_ref, *o_refs, splits, nchunk):
    hn = _rms(h_ref[...], g_ref[...]).astype(BF16)
    col = 0
    for o_ref, width in zip(o_refs, splits):
        for c0 in range(0, width, nchunk):
            cw = min(nchunk, width - c0)
            o_ref[:, c0:c0 + cw] = _dot(hn, w_ref[:, col + c0:col + c0 + cw]).astype(o_ref.dtype)
        col += width


def norm_proj(h, g, w, splits, dtypes, tm=512, nchunk=512):
    m, k = h.shape
    n = w.shape[1]
    assert sum(splits) == n and m % tm == 0
    return pl.pallas_call(
        functools.partial(_norm_proj_kernel, splits=tuple(splits), nchunk=nchunk),
        grid=(m // tm,),
        in_specs=[pl.BlockSpec((tm, k), lambda i: (i, 0)),
                  pl.BlockSpec((1, k), lambda i: (0, 0)),
                  pl.BlockSpec((k, n), lambda i: (0, 0))],
        out_specs=[pl.BlockSpec((tm, s), lambda i: (i, 0)) for s in splits],
        out_shape=[jax.ShapeDtypeStruct((m, s), d) for s, d in zip(splits, dtypes)],
        compiler_params=_cparams(("parallel",)),
        name="norm_proj",
    )(h, g.reshape(1, k), w)


def _proj_res_kernel(*refs):
    *a_refs, w_ref, h_ref, o_ref = refs
    acc = h_ref[...]
    k0 = 0
    for a_ref in a_refs:
        k = a_ref.shape[1]
        acc = acc + _dot(a_ref[...], w_ref[k0:k0 + k, :])
        k0 += k
    o_ref[...] = acc


def proj_residual(parts, w, h, tm=512):
    m = h.shape[0]
    n = w.shape[1]
    assert sum(a.shape[1] for a in parts) == w.shape[0]
    return pl.pallas_call(
        _proj_res_kernel,
        grid=(m // tm,),
        in_specs=[pl.BlockSpec((tm, a.shape[1]), lambda i: (i, 0)) for a in parts]
                 + [pl.BlockSpec(w.shape, lambda i: (0, 0)),
                    pl.BlockSpec((tm, n), lambda i: (i, 0))],
        out_specs=pl.BlockSpec((tm, n), lambda i: (i, 0)),
        out_shape=jax.ShapeDtypeStruct((m, n), F32),
        compiler_params=_cparams(("parallel",)),
        name="proj_residual",
    )(*parts, w, h)


def _ffn_kernel(h_ref, g_ref, wup_ref, cw_ref, cb_ref, wdn_ref, o_ref, hn_ref, act_ref, halo_ref,
                *, tiles_per_seq):
    tm = h_ref.shape[0]
    x = h_ref[...]
    hn_ref[...] = _rms(x, g_ref[...]).astype(BF16)
    seq_start = (pl.program_id(0) % tiles_per_seq) == 0
    row = lax.broadcasted_iota(jnp.int32, (tm, FF_CHUNK), 0)

    def conv_cols(c0):
        cols = slice(c0, c0 + FF_CHUNK)
        u = _dot(hn_ref[...], wup_ref[:, cols])
        prev = jnp.where(seq_start, 0.0, halo_ref[:, cols])
        halo_ref[:, cols] = u[tm - SUBLANES:, :]
        p1 = prev[SUBLANES - 1:SUBLANES, :]
        p2 = prev[SUBLANES - 2:SUBLANES - 1, :]
        u1 = jnp.where(row == 0, p1, pltpu.roll(u, 1, 0))
        u2 = jnp.where(row == 0, p2, jnp.where(row == 1, p1, pltpu.roll(u, 2, 0)))
        cw = cw_ref[:, cols]
        return cw[0:1, :] * u2 + cw[1:2, :] * u1 + cw[2:3, :] * u + cb_ref[:, cols]

    for c in range(D_FF // FF_CHUNK):
        gate = conv_cols(c * FF_CHUNK)
        up = conv_cols(D_FF + c * FF_CHUNK)
        act_ref[:, c * FF_CHUNK:(c + 1) * FF_CHUNK] = (gate * jax.nn.sigmoid(gate) * up).astype(BF16)
    o_ref[...] = x + _dot(act_ref[...], wdn_ref[...])


def conv_glu_ffn(h, g, w_up, conv_w, conv_b, w_down, seq, tm=512):
    m, d = h.shape
    f2 = w_up.shape[1]
    assert seq % tm == 0 and D_FF % FF_CHUNK == 0
    const = lambda i: (0, 0)
    return pl.pallas_call(
        functools.partial(_ffn_kernel, tiles_per_seq=seq // tm),
        grid=(m // tm,),
        in_specs=[pl.BlockSpec((tm, d), lambda i: (i, 0)),
                  pl.BlockSpec((1, d), const),
                  pl.BlockSpec((d, f2), const, pipeline_mode=pl.Buffered(1)),
                  pl.BlockSpec((CONV_W, f2), const),
                  pl.BlockSpec((1, f2), const),
                  pl.BlockSpec((D_FF, d), const, pipeline_mode=pl.Buffered(1))],
        out_specs=pl.BlockSpec((tm, d), lambda i: (i, 0)),
        out_shape=jax.ShapeDtypeStruct((m, d), F32),
        scratch_shapes=[pltpu.VMEM((tm, d), BF16),
                        pltpu.VMEM((tm, D_FF), BF16),
                        pltpu.VMEM((SUBLANES, f2), F32)],
        compiler_params=_cparams(("arbitrary",)),
        name="conv_glu_ffn",
    )(h, g.reshape(1, d), w_up, conv_w, conv_b.reshape(1, f2), w_down)


def _ple_kernel(h_ref, g_ref, wg_ref, p_ref, wp_ref, o_ref):
    x = h_ref[...]
    hn = _rms(x, g_ref[...]).astype(BF16)
    gate = jax.nn.sigmoid(_dot(hn, wg_ref[...]))
    o_ref[...] = x + gate * _dot(p_ref[...].astype(BF16), wp_ref[...])


def ple(h, g, w_gate, p, w_proj, tm=512):
    m, d = h.shape
    pd = p.shape[1]
    const = lambda i: (0, 0)
    return pl.pallas_call(
        _ple_kernel,
        grid=(m // tm,),
        in_specs=[pl.BlockSpec((tm, d), lambda i: (i, 0)),
                  pl.BlockSpec((1, d), const),
                  pl.BlockSpec((d, d), const),
                  pl.BlockSpec((tm, pd), lambda i: (i, 0)),
                  pl.BlockSpec((pd, d), const)],
        out_specs=pl.BlockSpec((tm, d), lambda i: (i, 0)),
        out_shape=jax.ShapeDtypeStruct((m, d), F32),
        compiler_params=_cparams(("parallel",)),
        name="ple",
    )(h, g.reshape(1, d), w_gate, p, w_proj)


def _sb_tri():
    j = np.arange(2 * BLK)[:, None] % BLK
    s = np.arange(2 * BLK)[None, :]
    return ((s >= BLK) | (j >= s)).astype(np.float32)


def _neg_abs(x):
    sign = jnp.uint32(0x80000000)
    return lax.bitcast_convert_type(lax.bitcast_convert_type(x, jnp.uint32) | sign, F32)


def _sb_kernel(q_ref, k_ref, v_ref, tri_ref, o_ref, qs_ref, acc_ref, c_ref):
    n = pl.program_id(1)
    npair = SB_WIDTH // LANES
    pairs = range(npair)
    cols = [slice(p * LANES, (p + 1) * LANES) for p in pairs]
    lane = lax.broadcasted_iota(jnp.int32, (2 * BLK, BLK), 1)
    row = lax.broadcasted_iota(jnp.int32, (2 * BLK, BLK), 0)
    first_head = row < BLK
    causal = lane < jnp.where(first_head, row, row - BLK)
    own_lanes = first_head == (lane < SB_DIM)

    for p in pairs:
        q2 = q_ref[0, :, cols[p]] * (SB_DIM ** -0.5)
        q2 = jnp.concatenate([q2, q2], axis=0)
        qs_ref[p] = jnp.where(own_lanes, q2, jnp.zeros_like(q2))

    def stacked(ref, p, r):
        if r == BLK:
            return ref[p]
        return jnp.concatenate([ref[p, 0:r], ref[p, BLK:BLK + r]], axis=0)

    def process(kb, nblk, diagonal, r):
        krows = pl.ds(pl.multiple_of(kb * BLK, BLK), nblk * BLK)
        first_half = lax.broadcasted_iota(jnp.int32, (r, BLK), 1) < SB_DIM
        qs = [stacked(qs_ref, p, r) for p in pairs]
        tn = [_dot_nt(qs[p], k_ref[0, krows, cols[p]]) * (-LOG2E) for p in pairs]
        order = list(range(nblk - 1, -1, -1))
        split = {}
        for p in pairs:
            for j in order:
                t = tn[p][:, j * BLK:(j + 1) * BLK]
                log_keep = jnp.minimum(t, 0.0) - jnp.log2(1.0 + jnp.exp2(_neg_abs(t)))
                if diagonal and j == nblk - 1:
                    log_keep = jnp.where(causal, log_keep, 0.0)
                split[p, j] = jnp.concatenate(_split_bf16(log_keep), axis=1)
        sums = {(p, j): _dot(split[p, j], tri_ref[...]) for p in pairs for j in order}
        ws = []
        for p in pairs:
            c = None if diagonal else stacked(c_ref, p, r)
            w = [None] * nblk
            for j in order:
                incl, total = sums[p, j][:, :BLK], sums[p, j][:, BLK:]
                arg = incl - tn[p][:, j * BLK:(j + 1) * BLK]
                if c is not None:
                    arg = arg + c
                wj = jnp.exp2(arg)
                if diagonal and j == nblk - 1:
                    wj = jnp.where(causal, wj, 0.0)
                w[j] = wj.astype(BF16)
                c = total if c is None else c + total
            ws.append(w[0] if nblk == 1 else jnp.concatenate(w, axis=1))
            if r == BLK:
                c_ref[p] = c
            else:
                c_ref[p, 0:r] = c[:r]
                c_ref[p, BLK:BLK + r] = c[r:]
        pvs = [_dot(ws[p], v_ref[0, krows, cols[p]]) for p in pairs]
        for p in pairs:
            pv = jnp.where(first_half, pvs[p][:r], pvs[p][r:])
            if diagonal:
                acc_ref[p] = pv
            else:
                acc_ref[p, 0:r] += pv

    @pl.when(n == 0)
    def _():
        process(n, 1, True, BLK)

    @pl.when(n > 0)
    def _():
        process(n - 1, 2, True, BLK)

    def live(lo, hi):
        c = c_ref[...]
        return jnp.max(jnp.maximum(c[:, lo:hi], c[:, BLK + lo:BLK + hi])) > SB_DEAD_LOG2

    def full_block(state):
        process(state[0], 1, False, BLK)
        return state[0] - 1, live(SB_NEAR_ROWS, BLK)

    def near_block(state):
        process(state[0], 1, False, SB_NEAR_ROWS)
        return state[0] - 1, live(0, SB_NEAR_ROWS)

    more = lambda st: (st[0] >= 0) & st[1]
    kb, _ = lax.while_loop(more, full_block, (n - 2, live(SB_NEAR_ROWS, BLK)))
    lax.while_loop(more, near_block, (kb, live(0, SB_NEAR_ROWS)))
    for p in pairs:
        o_ref[0, :, cols[p]] = acc_ref[p].astype(o_ref.dtype)


def stick_breaking(proj):
    b, s, _ = proj.shape
    tri = jnp.asarray(_sb_tri(), BF16)
    npair = SB_WIDTH // LANES
    return pl.pallas_call(
        _sb_kernel,
        grid=(b, s // BLK),
        in_specs=[pl.BlockSpec((1, BLK, SB_WIDTH), lambda i, n: (i, n, 0)),
                  pl.BlockSpec((1, s, SB_WIDTH), lambda i, n: (i, 0, 1)),
                  pl.BlockSpec((1, s, SB_WIDTH), lambda i, n: (i, 0, 2)),
                  pl.BlockSpec((2 * BLK, 2 * BLK), lambda i, n: (0, 0))],
        out_specs=pl.BlockSpec((1, BLK, SB_WIDTH), lambda i, n: (i, n, 0)),
        out_shape=jax.ShapeDtypeStruct((b, s, SB_WIDTH), BF16),
        scratch_shapes=[pltpu.VMEM((npair, 2 * BLK, LANES), BF16),
                        pltpu.VMEM((npair, BLK, LANES), F32),
                        pltpu.VMEM((npair, 2 * BLK, LANES), F32)],
        compiler_params=_cparams(("parallel", "arbitrary")),
        name="stick_breaking",
    )(proj, proj, proj, tri)


def _hg_time_mats():
    t = np.arange(BLK)[:, None]
    j = np.arange(BLK)[None, :]
    same = (t // HG_CHUNK) == (j // HG_CHUNK)
    per_chunk = np.arange(2 * SUBLANES)[:, None] == (j // HG_CHUNK)
    return np.concatenate([same & (j <= t), same, per_chunk], axis=0).astype(np.float32)


def _hgrn_kernel(q_ref, f_ref, i_ref, g_ref, lb_ref, gn_ref, tmat_ref, o_ref,
                 qf_s, kk_s, b_s, iv_s, qt_s, kt_s, dec_s, raw_s, state_s):
    ts = q_ref.shape[1]
    heads = range(HG_HEADS)
    hcols = [slice(h * LANES, (h + 1) * LANES) for h in heads]
    half = HG_CHUNK // 2
    cpb = BLK // HG_CHUNK

    @pl.when(pl.program_id(1) == 0)
    def _():
        state_s[...] = jnp.zeros_like(state_s)

    def prep(r, _):
        rows = pl.ds(pl.multiple_of(r * BLK, BLK), BLK)
        for h in heads:
            lb = lb_ref[:, hcols[h]]
            fp = f_ref[0, rows, hcols[h]]
            q = q_ref[0, rows, hcols[h]]
            e = jnp.exp(-jnp.abs(fp))
            sg = 1.0 / (1.0 + e)
            log_f = jnp.log(lb + (1.0 - lb) * jnp.where(fp >= 0, sg, e * sg))
            kk = (1.0 - lb) * jnp.where(fp >= 0, e * sg, sg)
            qf = q * jax.nn.sigmoid(q)
            hi, lo = _split_bf16(log_f)
            sums = _dot(tmat_ref[...], jnp.concatenate([hi, lo], axis=1))
            sums = sums[:, :LANES] + sums[:, LANES:]
            b = sums[:BLK]
            b_tot = sums[BLK:2 * BLK]
            chunk_tot = sums[2 * BLK:2 * BLK + cpb]
            qf_s[h, rows, :] = qf
            kk_s[h, rows, :] = kk
            b_s[h, rows, :] = b
            iv_s[h, rows, :] = i_ref[0, rows, hcols[h]]
            qt_s[h, rows, :] = (qf * jnp.exp(b)).astype(BF16)
            kt_s[h, rows, :] = (kk * jnp.exp(b_tot - b)).astype(BF16)
            dec_s[h, pl.ds(pl.multiple_of(r * cpb, cpb), cpb), :] = jnp.exp(chunk_tot)
        return 0

    lax.fori_loop(0, ts // BLK, prep, 0)

    rowi = lax.broadcasted_iota(jnp.int32, (half, 1), 0)

    def chunk(c, _):
        r0 = pl.multiple_of(c * HG_CHUNK, HG_CHUNK)
        rows = pl.ds(r0, HG_CHUNK)
        lo_rows = pl.ds(r0, half)
        hi_rows = pl.ds(r0 + half, half)
        inter = [_dot_nt(qt_s[h, rows, :], state_s[h].astype(BF16)) for h in heads]
        for h in heads:
            cols = hcols[h]
            qf_lo, qf_hi = qf_s[h, lo_rows, :], qf_s[h, hi_rows, :]
            b_lo, b_hi = b_s[h, lo_rows, :], b_s[h, hi_rows, :]
            o_lo = o_hi = jnp.zeros((half, LANES), F32)
            for s in range(HG_CHUNK):
                one = pl.ds(r0 + s, 1)
                b_key, k_key, v_key = b_s[h, one, :], kk_s[h, one, :], iv_s[h, one, :]
                if s < half:
                    score = jnp.sum(qf_lo * k_key * jnp.exp(b_lo - b_key), axis=-1, keepdims=True)
                    if s > 0:
                        score = jnp.where(rowi >= s, score, 0.0)
                    o_lo = o_lo + score * v_key
                score = jnp.sum(qf_hi * k_key * jnp.exp(b_hi - b_key), axis=-1, keepdims=True)
                if s > half:
                    score = jnp.where(rowi >= s - half, score, 0.0)
                o_hi = o_hi + score * v_key
            raw_s[h, lo_rows, :] = o_lo + inter[h][:half]
            raw_s[h, hi_rows, :] = o_hi + inter[h][half:]
        upd = [_dot_tn(iv_s[h, rows, :].astype(BF16), kt_s[h, rows, :]) for h in heads]
        for h in heads:
            state_s[h] = state_s[h] * dec_s[h, pl.ds(c, 1), :] + upd[h]
        return 0

    lax.fori_loop(0, ts // HG_CHUNK, chunk, 0)

    def finish(r, _):
        rows = pl.ds(pl.multiple_of(r * BLK, BLK), BLK)
        for h in heads:
            g = g_ref[0, rows, hcols[h]]
            y = _rms(raw_s[h, rows, :], gn_ref[...]) * (g * jax.nn.sigmoid(g))
            o_ref[0, rows, hcols[h]] = y.astype(o_ref.dtype)
        return 0

    lax.fori_loop(0, ts // BLK, finish, 0)


def hgrn2(proj, lb, out_norm, ts=512):
    b, s, _ = proj.shape
    assert s % ts == 0 and ts % BLK == 0
    tmat = jnp.asarray(_hg_time_mats(), BF16)
    seq_blk = lambda part: pl.BlockSpec((1, ts, HG_QK), lambda i, t: (i, t, part))
    const = lambda i, t: (0, 0)
    return pl.pallas_call(
        _hgrn_kernel,
        grid=(b, s // ts),
        in_specs=[seq_blk(0), seq_blk(1), seq_blk(2), seq_blk(3),
                  pl.BlockSpec((1, HG_QK), const),
                  pl.BlockSpec((1, HG_DV), const),
                  pl.BlockSpec(tmat.shape, const)],
        out_specs=pl.BlockSpec((1, ts, HG_V), lambda i, t: (i, t, 0)),
        out_shape=jax.ShapeDtypeStruct((b, s, HG_V), BF16),
        scratch_shapes=[pltpu.VMEM((HG_HEADS, ts, HG_DK), F32), pltpu.VMEM((HG_HEADS, ts, HG_DK), F32),
                        pltpu.VMEM((HG_HEADS, ts, HG_DK), F32), pltpu.VMEM((HG_HEADS, ts, HG_DV), F32),
                        pltpu.VMEM((HG_HEADS, ts, HG_DK), BF16), pltpu.VMEM((HG_HEADS, ts, HG_DK), BF16),
                        pltpu.VMEM((HG_HEADS, ts // HG_CHUNK, HG_DK), F32),
                        pltpu.VMEM((HG_HEADS, ts, HG_DV), F32),
                        pltpu.VMEM((HG_HEADS, HG_DV, HG_DK), F32)],
        compiler_params=_cparams(("parallel", "arbitrary")),
        name="hgrn2",
    )(proj, proj, proj, proj, lb.reshape(1, HG_QK), out_norm.reshape(1, HG_DV), tmat)


def _t5_band_buckets():
    t = np.arange(WINDOW)[:, None]
    s = np.arange(2 * WINDOW)[None, :]
    dist = t + WINDOW - s
    max_exact = N_BUCKETS // 2
    large = max_exact + (np.log(np.maximum(dist, max_exact) / max_exact) / math.log(MAX_DISTANCE / max_exact)
                         * (N_BUCKETS - max_exact)).astype(np.int32)
    large = np.minimum(large, N_BUCKETS - 1)
    band = (dist >= 0) & (dist < WINDOW)
    return np.where(dist < max_exact, np.maximum(dist, 0), large).astype(np.int32), band


def _swa_kernel(sink_ref, q_ref, kc_ref, kp_ref, vc_ref, vp_ref, bias_ref, qg_ref, kg_ref, gmat_ref, o_ref):
    n = pl.program_id(1)
    w2 = 2 * WINDOW
    no_prev = jnp.where(n > 0, 0.0, -jnp.inf)
    q_groups = SW_HEADS * SW_DIM // LANES
    kv_groups = SW_KV_HEADS * SW_DIM // LANES
    per_kv = q_groups // kv_groups
    grp = lambda ref, p: ref[0, :, p * LANES:(p + 1) * LANES]

    def head_norm(x, gain):
        sq = jnp.concatenate(_split_bf16(x * x), axis=1)
        ssq = _dot(sq, gmat_ref[...])
        return x * lax.rsqrt(ssq * (1.0 / SW_DIM) + EPS) * gain

    qn = head_norm(jnp.concatenate([grp(q_ref, p) for p in range(q_groups)], axis=0), qg_ref[...])
    kn = head_norm(jnp.concatenate([x for r in range(kv_groups) for x in (grp(kp_ref, r), grp(kc_ref, r))], axis=0),
                   kg_ref[...])
    v = jnp.concatenate([x for r in range(kv_groups) for x in (grp(vp_ref, r), grp(vc_ref, r))], axis=0)
    kn_rot, v_rot = pltpu.roll(kn, SW_DIM, 1), pltpu.roll(v, SW_DIM, 1)
    kn, kn_rot, v, v_rot = (a.astype(BF16) for a in (kn, kn_rot, v, v_rot))
    ones = jnp.ones((w2, LANES), BF16)

    rows_q = per_kv * WINDOW
    lane = lax.broadcasted_iota(jnp.int32, (rows_q, LANES), 1)
    row = lax.broadcasted_iota(jnp.int32, (rows_q, LANES), 0)
    same_half = (lane < SW_DIM) == (row < rows_q // 2)

    logits = []
    for r in range(kv_groups):
        qr = qn[r * rows_q:(r + 1) * rows_q]
        keys = slice(r * w2, (r + 1) * w2)
        zero = jnp.zeros_like(qr)
        logits.append((_dot_nt(jnp.where(same_half, qr, zero).astype(BF16), kn[keys]),
                       _dot_nt(jnp.where(same_half, zero, qr).astype(BF16), kn_rot[keys])))

    weights, sink_terms = [], []
    for r in range(kv_groups):
        w_r, s_r = ([], []), ([], [])
        for j in range(per_kv):
            p = r * per_kv + j
            second = j >= per_kv // 2
            for variant, h in ((0, 2 * p + int(second)), (1, 2 * p + int(not second))):
                lg = logits[r][variant][j * WINDOW:(j + 1) * WINDOW]
                lp = lg[:, :WINDOW] + (bias_ref[h, :, :WINDOW] + no_prev)
                lc = lg[:, WINDOW:] + bias_ref[h, :, WINDOW:]
                sink = sink_ref[h]
                m = jnp.maximum(jnp.max(jnp.maximum(lp, lc), axis=-1, keepdims=True), sink)
                w_r[variant].append(jnp.concatenate([jnp.exp(lp - m), jnp.exp(lc - m)], axis=1).astype(BF16))
                s_r[variant].append(jnp.broadcast_to(jnp.exp(sink - m), (WINDOW, LANES)))
        weights.append(tuple(jnp.concatenate(x, axis=0) for x in w_r))
        sink_terms.append(tuple(jnp.concatenate(x, axis=0) for x in s_r))

    outs = []
    for r in range(kv_groups):
        keys = slice(r * w2, (r + 1) * w2)
        outs.append((_dot(weights[r][0], jnp.concatenate([v[keys], ones], axis=1)),
                     _dot(weights[r][1], jnp.concatenate([v_rot[keys], ones], axis=1))))
    for r in range(kv_groups):
        o = [outs[r][i][:, :LANES] / (outs[r][i][:, LANES:] + sink_terms[r][i]) for i in range(2)]
        o = jnp.where(same_half, o[0], o[1]).astype(o_ref.dtype)
        for j in range(per_kv):
            p = r * per_kv + j
            o_ref[0, :, p * LANES:(p + 1) * LANES] = o[j * WINDOW:(j + 1) * WINDOW]


def _swa_group_sum():
    i = np.arange(2 * LANES)[:, None] % LANES
    j = np.arange(LANES)[None, :]
    return (i // SW_DIM == j // SW_DIM).astype(np.float32)


def sliding_window(proj, q_norm, k_norm, sinks, rel_bias):
    b, s, _ = proj.shape
    bucket, band = _t5_band_buckets()
    onehot = jnp.asarray(bucket[..., None] == np.arange(N_BUCKETS), F32)
    bias = jnp.einsum('tsb,bh->hts', onehot, rel_bias.astype(F32), precision=lax.Precision.HIGHEST)
    bias = jnp.where(band, bias, -jnp.inf)
    kv_w = SW_KV_HEADS * SW_DIM
    q_w = SW_HEADS * SW_DIM
    k_blk = q_w // kv_w
    per_lane_group = LANES // SW_DIM
    q_gain = jnp.tile(q_norm.astype(F32), per_lane_group).reshape(1, LANES) * (SW_DIM ** -0.5)
    k_gain = jnp.tile(k_norm.astype(F32), per_lane_group).reshape(1, LANES)
    prev = lambda n: jnp.maximum(n - 1, 0)
    const2 = lambda i, n: (0, 0)
    return pl.pallas_call(
        _swa_kernel,
        grid=(b, s // WINDOW),
        in_specs=[pl.BlockSpec(memory_space=pltpu.SMEM),
                  pl.BlockSpec((1, WINDOW, q_w), lambda i, n: (i, n, 0)),
                  pl.BlockSpec((1, WINDOW, kv_w), lambda i, n: (i, n, k_blk)),
                  pl.BlockSpec((1, WINDOW, kv_w), lambda i, n: (i, prev(n), k_blk)),
                  pl.BlockSpec((1, WINDOW, kv_w), lambda i, n: (i, n, k_blk + 1)),
                  pl.BlockSpec((1, WINDOW, kv_w), lambda i, n: (i, prev(n), k_blk + 1)),
                  pl.BlockSpec((SW_HEADS, WINDOW, 2 * WINDOW), lambda i, n: (0, 0, 0)),
                  pl.BlockSpec((1, LANES), const2),
                  pl.BlockSpec((1, LANES), const2),
                  pl.BlockSpec((2 * LANES, LANES), const2)],
        out_specs=pl.BlockSpec((1, WINDOW, q_w), lambda i, n: (i, n, 0)),
        out_shape=jax.ShapeDtypeStruct((b, s, q_w), BF16),
        compiler_params=_cparams(("parallel", "parallel")),
        name="sliding_window",
    )(sinks.astype(F32), proj, proj, proj, proj, proj, bias, q_gain, k_gain,
      jnp.asarray(_swa_group_sum(), BF16))


def kernel(x, p, mix_norm, ab_w_in, hg_lb_logits, hg_out_norm, ab_w_out, c_w_in, q_norm, k_norm, sinks,
           rel_bias, c_w_out, ffn_norm, ffn_up, ffn_conv, ffn_conv_b, ffn_down, ple_norm, ple_gate, ple_proj):
    b, s, d = x.shape
    depth = p.shape[0]
    m = b * s
    lb_cum = jnp.cumsum(jax.nn.softmax(hg_lb_logits.astype(F32), axis=0), axis=0)
    lower_bounds = lb_cum - lb_cum[0]
    h = x.reshape(m, d)
    for i in range(depth):
        j = i // 2
        if i % 2 == 0:
            att, gates = norm_proj(h, mix_norm[i], ab_w_in[j].astype(BF16),
                                   (3 * SB_WIDTH, 2 * HG_QK + 2 * HG_V), (BF16, F32))
            o_a = stick_breaking(att.reshape(b, s, -1))
            o_b = hgrn2(gates.reshape(b, s, -1), lower_bounds[j], hg_out_norm[j])
            h = proj_residual([o_a.reshape(m, -1), o_b.reshape(m, -1)], ab_w_out[j].astype(BF16), h)
        else:
            (qkv,) = norm_proj(h, mix_norm[i], c_w_in[j].astype(BF16), (c_w_in.shape[2],), (F32,))
            o = sliding_window(qkv.reshape(b, s, -1), q_norm[j], k_norm[j], sinks[j], rel_bias)
            h = proj_residual([o.reshape(m, -1)], c_w_out[j].astype(BF16), h)
        h = conv_glu_ffn(h, ffn_norm[i], ffn_up[i].astype(BF16), ffn_conv[i], ffn_conv_b[i],
                         ffn_down[i].astype(BF16), s)
        h = ple(h, ple_norm[i], ple_gate[i].astype(BF16), p[i].reshape(m, -1), ple_proj[i].astype(BF16))
    return h.reshape(b, s, d)
```

```python
import functools
import math

import numpy as np
import jax
import jax.numpy as jnp
from jax import lax
from jax.experimental import pallas as pl
from jax.experimental.pallas import tpu as pltpu

F32 = jnp.float32
BF16 = jnp.bfloat16

D_MODEL = 1024
PLE_DIM = 256
EPS = 1e-6
SB_HEADS = 8
SB_DIM = 64
SB_WIDTH = SB_HEADS * SB_DIM
HG_HEADS = 4
HG_DK = 128
HG_DV = 128
HG_QK = HG_HEADS * HG_DK
HG_V = HG_HEADS * HG_DV
SW_HEADS = 16
SW_KV_HEADS = 4
SW_DIM = 64
SW_GROUP = SW_HEADS // SW_KV_HEADS
WINDOW = 128
N_BUCKETS = 32
MAX_DISTANCE = 128
D_FF = 2816
CONV_W = 3

LANES = 128
SUBLANES = 8
BLK = 128
HG_CHUNK = 16
FF_CHUNK = 256
SB_DEAD_LOG2 = -150.0
SB_NEAR_ROWS = 32
LOG2E = 1.4426950408889634
VMEM_LIMIT = 56 * 1024 * 1024


def _cparams(sem):
    return pltpu.CompilerParams(dimension_semantics=sem, vmem_limit_bytes=VMEM_LIMIT)


def _rms(x, g):
    ms = jnp.mean(x * x, axis=-1, keepdims=True)
    return x * lax.rsqrt(ms + EPS) * g


def _dot(a, b):
    return jnp.dot(a, b, preferred_element_type=F32)


def _dot_nt(a, b):
    return lax.dot_general(a, b, (((1,), (1,)), ((), ())), preferred_element_type=F32)


def _dot_tn(a, b):
    return lax.dot_general(a, b, (((0,), (0,)), ((), ())), preferred_element_type=F32)


def _split_bf16(x):
    hi = x.astype(BF16)
    lo = (x - hi.astype(F32)).astype(BF16)
    return hi, lo


def _layer_spec(stack, layer, single_buffer=False):
    zeros = (0,) * (stack.ndim - 1)
    kwargs = dict(pipeline_mode=pl.Buffered(1)) if single_buffer else {}
    return pl.BlockSpec((None,) + stack.shape[1:], lambda *_: (layer,) + zeros, **kwargs)


def _row_spec(tm, width):
    return pl.BlockSpec((tm, width), lambda i: (i, 0))


def _project(hn, w_ref, o_refs, splits, nchunk):
    col = 0
    for o_ref, width in zip(o_refs, splits):
        for c0 in range(0, width, nchunk):
            cw = min(nchunk, width - c0)
            o_ref[:, c0:c0 + cw] = _dot(hn, w_ref[:, col + c0:col + c0 + cw]).astype(o_ref.dtype)
        col += width


def _norm_proj_kernel(h_ref, g_ref, w_ref, *o_refs, splits, nchunk):
    _project(_rms(h_ref[...], g_ref[...]).astype(BF16), w_ref, o_refs, splits, nchunk)


def norm_proj(h, g, w_stack, layer, splits, dtypes, tm=512, nchunk=512):
    m, k = h.shape
    assert sum(splits) == w_stack.shape[2] and m % tm == 0
    return pl.pallas_call(
        functools.partial(_norm_proj_kernel, splits=tuple(splits), nchunk=nchunk),
        grid=(m // tm,),
        in_specs=[_row_spec(tm, k), pl.BlockSpec((1, k), lambda i: (0, 0)), _layer_spec(w_stack, layer)],
        out_specs=[_row_spec(tm, s) for s in splits],
        out_shape=[jax.ShapeDtypeStruct((m, s), d) for s, d in zip(splits, dtypes)],
        compiler_params=_cparams(("parallel",)),
        name="norm_proj",
    )(h, g.reshape(1, k), w_stack)


def _mix_ffn_kernel(*refs, n_parts, tiles_per_seq):
    a_refs = refs[:n_parts]
    wo_ref, h_ref, g_ref, wup_ref, cw_ref, cb_ref, wdn_ref, o_ref, hn_ref, act_ref, halo_ref = refs[n_parts:]
    tm = h_ref.shape[0]
    x = h_ref[...]
    k0 = 0
    for a_ref in a_refs:
        k = a_ref.shape[1]
        x = x + _dot(a_ref[...], wo_ref[k0:k0 + k, :])
        k0 += k
    hn_ref[...] = _rms(x, g_ref[...]).astype(BF16)
    seq_start = (pl.program_id(0) % tiles_per_seq) == 0
    row = lax.broadcasted_iota(jnp.int32, (tm, FF_CHUNK), 0)

    def conv_cols(c0):
        cols = slice(c0, c0 + FF_CHUNK)
        u = _dot(hn_ref[...], wup_ref[:, cols])
        prev = jnp.where(seq_start, 0.0, halo_ref[:, cols])
        halo_ref[:, cols] = u[tm - SUBLANES:, :]
        p1 = prev[SUBLANES - 1:SUBLANES, :]
        p2 = prev[SUBLANES - 2:SUBLANES - 1, :]
        u1 = jnp.where(row == 0, p1, pltpu.roll(u, 1, 0))
        u2 = jnp.where(row == 0, p2, jnp.where(row == 1, p1, pltpu.roll(u, 2, 0)))
        cw = cw_ref[:, cols]
        return cw[0:1, :] * u2 + cw[1:2, :] * u1 + cw[2:3, :] * u + cb_ref[:, cols]

    for c in range(D_FF // FF_CHUNK):
        gate = conv_cols(c * FF_CHUNK)
        up = conv_cols(D_FF + c * FF_CHUNK)
        act_ref[:, c * FF_CHUNK:(c + 1) * FF_CHUNK] = (gate * jax.nn.sigmoid(gate) * up).astype(BF16)
    o_ref[...] = x + _dot(act_ref[...], wdn_ref[...])


def mix_ffn(parts, w_out, out_layer, h, g, w_up, conv_w, conv_b, w_down, layer, seq, tm=512):
    m, d = h.shape
    f2 = w_up.shape[2]
    assert seq % tm == 0 and D_FF % FF_CHUNK == 0 and sum(a.shape[1] for a in parts) == w_out.shape[1]
    return pl.pallas_call(
        functools.partial(_mix_ffn_kernel, n_parts=len(parts), tiles_per_seq=seq // tm),
        grid=(m // tm,),
        in_specs=[_row_spec(tm, a.shape[1]) for a in parts]
                 + [_layer_spec(w_out, out_layer, True), _row_spec(tm, d), pl.BlockSpec((1, d), lambda i: (0, 0)),
                    _layer_spec(w_up, layer, True), _layer_spec(conv_w, layer), _layer_spec(conv_b, layer),
                    _layer_spec(w_down, layer, True)],
        out_specs=_row_spec(tm, d),
        out_shape=jax.ShapeDtypeStruct((m, d), F32),
        scratch_shapes=[pltpu.VMEM((tm, d), BF16),
                        pltpu.VMEM((tm, D_FF), BF16),
                        pltpu.VMEM((SUBLANES, f2), F32)],
        compiler_params=_cparams(("arbitrary",)),
        name="mix_ffn",
    )(*parts, w_out, h, g.reshape(1, d), w_up, conv_w, conv_b, w_down)


def _ple_proj_kernel(*refs, splits, nchunk):
    h_ref, g_ref, wg_ref, p_ref, wp_ref = refs[:5]
    x = h_ref[...]
    gate = jax.nn.sigmoid(_dot(_rms(x, g_ref[...]).astype(BF16), wg_ref[...]))
    x = x + gate * _dot(p_ref[...].astype(BF16), wp_ref[...])
    if splits:
        g2_ref, win_ref, o_ref, *proj_refs = refs[5:]
        _project(_rms(x, g2_ref[...]).astype(BF16), win_ref, proj_refs, splits, nchunk)
    else:
        (o_ref,) = refs[5:]
    o_ref[...] = x


def ple_and_proj(h, g, w_gate, p, w_proj, layer, nxt=None, tm=512, nchunk=512):
    m, d = h.shape
    in_specs = [_row_spec(tm, d), pl.BlockSpec((1, d), lambda i: (0, 0)), _layer_spec(w_gate, layer, True),
                pl.BlockSpec((None, tm, p.shape[2]), lambda i: (layer, i, 0)), _layer_spec(w_proj, layer, True)]
    args = [h, g.reshape(1, d), w_gate, p, w_proj]
    out_specs = [_row_spec(tm, d)]
    out_shape = [jax.ShapeDtypeStruct((m, d), F32)]
    splits = ()
    if nxt is not None:
        g2, w_in, in_layer, splits, dtypes = nxt
        assert sum(splits) == w_in.shape[2]
        in_specs += [pl.BlockSpec((1, d), lambda i: (0, 0)), _layer_spec(w_in, in_layer, True)]
        args += [g2.reshape(1, d), w_in]
        out_specs += [_row_spec(tm, s) for s in splits]
        out_shape += [jax.ShapeDtypeStruct((m, s), dt) for s, dt in zip(splits, dtypes)]
    return pl.pallas_call(
        functools.partial(_ple_proj_kernel, splits=tuple(splits), nchunk=nchunk),
        grid=(m // tm,),
        in_specs=in_specs,
        out_specs=out_specs,
        out_shape=out_shape,
        compiler_params=_cparams(("parallel",)),
        name="ple_proj",
    )(*args)


def _sb_tri():
    j = np.arange(2 * BLK)[:, None] % BLK
    s = np.arange(2 * BLK)[None, :]
    return ((s >= BLK) | (j >= s)).astype(np.float32)


def _neg_abs(x):
    sign = jnp.uint32(0x80000000)
    return lax.bitcast_convert_type(lax.bitcast_convert_type(x, jnp.uint32) | sign, F32)


def _sb_kernel(q_ref, k_ref, v_ref, tri_ref, o_ref, qs_ref, acc_ref, c_ref):
    n = pl.program_id(1)
    npair = SB_WIDTH // LANES
    pairs = range(npair)
    cols = [slice(p * LANES, (p + 1) * LANES) for p in pairs]
    lane = lax.broadcasted_iota(jnp.int32, (2 * BLK, BLK), 1)
    row = lax.broadcasted_iota(jnp.int32, (2 * BLK, BLK), 0)
    first_head = row < BLK
    causal = lane < jnp.where(first_head, row, row - BLK)
    own_lanes = first_head == (lane < SB_DIM)

    for p in pairs:
        q2 = q_ref[0, :, cols[p]] * (SB_DIM ** -0.5)
        q2 = jnp.concatenate([q2, q2], axis=0)
        qs_ref[p] = jnp.where(own_lanes, q2, jnp.zeros_like(q2))

    def stacked(ref, p, r):
        if r == BLK:
            return ref[p]
        return jnp.concatenate([ref[p, 0:r], ref[p, BLK:BLK + r]], axis=0)

    def process(kb, nblk, diagonal, r):
        krows = pl.ds(pl.multiple_of(kb * BLK, BLK), nblk * BLK)
        first_half = lax.broadcasted_iota(jnp.int32, (r, BLK), 1) < SB_DIM
        qs = [stacked(qs_ref, p, r) for p in pairs]
        tn = [_dot_nt(qs[p], k_ref[0, krows, cols[p]]) * (-LOG2E) for p in pairs]
        order = list(range(nblk - 1, -1, -1))
        split = {}
        for p in pairs:
            for j in order:
                t = tn[p][:, j * BLK:(j + 1) * BLK]
                log_keep = jnp.minimum(t, 0.0) - jnp.log2(1.0 + jnp.exp2(_neg_abs(t)))
                if diagonal and j == nblk - 1:
                    log_keep = jnp.where(causal, log_keep, 0.0)
                split[p, j] = jnp.concatenate(_split_bf16(log_keep), axis=1)
        sums = {(p, j): _dot(split[p, j], tri_ref[...]) for p in pairs for j in order}
        ws = []
        for p in pairs:
            c = None if diagonal else stacked(c_ref, p, r)
            w = [None] * nblk
            for j in order:
                incl, total = sums[p, j][:, :BLK], sums[p, j][:, BLK:]
                arg = incl - tn[p][:, j * BLK:(j + 1) * BLK]
                if c is not None:
                    arg = arg + c
                wj = jnp.exp2(arg)
                if diagonal and j == nblk - 1:
                    wj = jnp.where(causal, wj, 0.0)
                w[j] = wj.astype(BF16)
                c = total if c is None else c + total
            ws.append(w[0] if nblk == 1 else jnp.concatenate(w, axis=1))
            if r == BLK:
                c_ref[p] = c
            else:
                c_ref[p, 0:r] = c[:r]
                c_ref[p, BLK:BLK + r] = c[r:]
        pvs = [_dot(ws[p], v_ref[0, krows, cols[p]]) for p in pairs]
        for p in pairs:
            pv = jnp.where(first_half, pvs[p][:r], pvs[p][r:])
            if diagonal:
                acc_ref[p] = pv
            else:
                acc_ref[p, 0:r] += pv

    @pl.when(n == 0)
    def _():
        process(n, 1, True, BLK)

    @pl.when(n > 0)
    def _():
        process(n - 1, 2, True, BLK)

    def live(lo, hi):
        c = c_ref[...]
        return jnp.max(jnp.maximum(c[:, lo:hi], c[:, BLK + lo:BLK + hi])) > SB_DEAD_LOG2

    def full_block(state):
        process(state[0], 1, False, BLK)
        return state[0] - 1, live(SB_NEAR_ROWS, BLK)

    def near_block(state):
        process(state[0], 1, False, SB_NEAR_ROWS)
        return state[0] - 1, live(0, SB_NEAR_ROWS)

    more = lambda st: (st[0] >= 0) & st[1]
    kb, _ = lax.while_loop(more, full_block, (n - 2, live(SB_NEAR_ROWS, BLK)))
    lax.while_loop(more, near_block, (kb, live(0, SB_NEAR_ROWS)))
    for p in pairs:
        o_ref[0, :, cols[p]] = acc_ref[p].astype(o_ref.dtype)


def stick_breaking(proj):
    b, s, _ = proj.shape
    tri = jnp.asarray(_sb_tri(), BF16)
    npair = SB_WIDTH // LANES
    return pl.pallas_call(
        _sb_kernel,
        grid=(b, s // BLK),
        in_specs=[pl.BlockSpec((1, BLK, SB_WIDTH), lambda i, n: (i, n, 0)),
                  pl.BlockSpec((1, s, SB_WIDTH), lambda i, n: (i, 0, 1)),
                  pl.BlockSpec((1, s, SB_WIDTH), lambda i, n: (i, 0, 2)),
                  pl.BlockSpec((2 * BLK, 2 * BLK), lambda i, n: (0, 0))],
        out_specs=pl.BlockSpec((1, BLK, SB_WIDTH), lambda i, n: (i, n, 0)),
        out_shape=jax.ShapeDtypeStruct((b, s, SB_WIDTH), BF16),
        scratch_shapes=[pltpu.VMEM((npair, 2 * BLK, LANES), BF16),
                        pltpu.VMEM((npair, BLK, LANES), F32),
                        pltpu.VMEM((npair, 2 * BLK, LANES), F32)],
        compiler_params=_cparams(("parallel", "arbitrary")),
        name="stick_breaking",
    )(proj, proj, proj, tri)


def _hg_time_mats():
    t = np.arange(BLK)[:, None]
    j = np.arange(BLK)[None, :]
    same = (t // HG_CHUNK) == (j // HG_CHUNK)
    per_chunk = np.arange(2 * SUBLANES)[:, None] == (j // HG_CHUNK)
    return np.concatenate([same & (j <= t), same, per_chunk], axis=0).astype(np.float32)


def _hgrn_kernel(q_ref, f_ref, i_ref, g_ref, lb_ref, gn_ref, tmat_ref, o_ref,
                 qf_s, kk_s, b_s, iv_s, qt_s, kt_s, dec_s, raw_s, state_s):
    ts = q_ref.shape[1]
    heads = range(HG_HEADS)
    hcols = [slice(h * LANES, (h + 1) * LANES) for h in heads]
    half = HG_CHUNK // 2
    cpb = BLK // HG_CHUNK

    @pl.when(pl.program_id(1) == 0)
    def _():
        state_s[...] = jnp.zeros_like(state_s)

    def prep(r, _):
        rows = pl.ds(pl.multiple_of(r * BLK, BLK), BLK)
        for h in heads:
            lb = lb_ref[:, hcols[h]]
            fp = f_ref[0, rows, hcols[h]]
            q = q_ref[0, rows, hcols[h]]
            e = jnp.exp(-jnp.abs(fp))
            sg = 1.0 / (1.0 + e)
            log_f = jnp.log(lb + (1.0 - lb) * jnp.where(fp >= 0, sg, e * sg))
            kk = (1.0 - lb) * jnp.where(fp >= 0, e * sg, sg)
            qf = q * jax.nn.sigmoid(q)
            hi, lo = _split_bf16(log_f)
            sums = _dot(tmat_ref[...], jnp.concatenate([hi, lo], axis=1))
            sums = sums[:, :LANES] + sums[:, LANES:]
            b = sums[:BLK]
            b_tot = sums[BLK:2 * BLK]
            chunk_tot = sums[2 * BLK:2 * BLK + cpb]
            qf_s[h, rows, :] = qf
            kk_s[h, rows, :] = kk
            b_s[h, rows, :] = b
            iv_s[h, rows, :] = i_ref[0, rows, hcols[h]]
            qt_s[h, rows, :] = (qf * jnp.exp(b)).astype(BF16)
            kt_s[h, rows, :] = (kk * jnp.exp(b_tot - b)).astype(BF16)
            dec_s[h, pl.ds(pl.multiple_of(r * cpb, cpb), cpb), :] = jnp.exp(chunk_tot)
        return 0

    lax.fori_loop(0, ts // BLK, prep, 0)

    rowi = lax.broadcasted_iota(jnp.int32, (half, 1), 0)

    def chunk(c, _):
        r0 = pl.multiple_of(c * HG_CHUNK, HG_CHUNK)
        rows = pl.ds(r0, HG_CHUNK)
        lo_rows = pl.ds(r0, half)
        hi_rows = pl.ds(r0 + half, half)
        inter = [_dot_nt(qt_s[h, rows, :], state_s[h].astype(BF16)) for h in heads]
        for h in heads:
            cols = hcols[h]
            qf_lo, qf_hi = qf_s[h, lo_rows, :], qf_s[h, hi_rows, :]
            b_lo, b_hi = b_s[h, lo_rows, :], b_s[h, hi_rows, :]
            o_lo = o_hi = jnp.zeros((half, LANES), F32)
            for s in range(HG_CHUNK):
                one = pl.ds(r0 + s, 1)
                b_key, k_key, v_key = b_s[h, one, :], kk_s[h, one, :], iv_s[h, one, :]
                if s < half:
                    score = jnp.sum(qf_lo * k_key * jnp.exp(b_lo - b_key), axis=-1, keepdims=True)
                    if s > 0:
                        score = jnp.where(rowi >= s, score, 0.0)
                    o_lo = o_lo + score * v_key
                score = jnp.sum(qf_hi * k_key * jnp.exp(b_hi - b_key), axis=-1, keepdims=True)
                if s > half:
                    score = jnp.where(rowi >= s - half, score, 0.0)
                o_hi = o_hi + score * v_key
            raw_s[h, lo_rows, :] = o_lo + inter[h][:half]
            raw_s[h, hi_rows, :] = o_hi + inter[h][half:]
        upd = [_dot_tn(iv_s[h, rows, :].astype(BF16), kt_s[h, rows, :]) for h in heads]
        for h in heads:
            state_s[h] = state_s[h] * dec_s[h, pl.ds(c, 1), :] + upd[h]
        return 0

    lax.fori_loop(0, ts // HG_CHUNK, chunk, 0)

    def finish(r, _):
        rows = pl.ds(pl.multiple_of(r * BLK, BLK), BLK)
        for h in heads:
            g = g_ref[0, rows, hcols[h]]
            y = _rms(raw_s[h, rows, :], gn_ref[...]) * (g * jax.nn.sigmoid(g))
            o_ref[0, rows, hcols[h]] = y.astype(o_ref.dtype)
        return 0

    lax.fori_loop(0, ts // BLK, finish, 0)


def hgrn2(proj, lb, out_norm, ts=512):
    b, s, _ = proj.shape
    assert s % ts == 0 and ts % BLK == 0
    tmat = jnp.asarray(_hg_time_mats(), BF16)
    seq_blk = lambda part: pl.BlockSpec((1, ts, HG_QK), lambda i, t: (i, t, part))
    const = lambda i, t: (0, 0)
    return pl.pallas_call(
        _hgrn_kernel,
        grid=(b, s // ts),
        in_specs=[seq_blk(0), seq_blk(1), seq_blk(2), seq_blk(3),
                  pl.BlockSpec((1, HG_QK), const),
                  pl.BlockSpec((1, HG_DV), const),
                  pl.BlockSpec(tmat.shape, const)],
        out_specs=pl.BlockSpec((1, ts, HG_V), lambda i, t: (i, t, 0)),
        out_shape=jax.ShapeDtypeStruct((b, s, HG_V), BF16),
        scratch_shapes=[pltpu.VMEM((HG_HEADS, ts, HG_DK), F32), pltpu.VMEM((HG_HEADS, ts, HG_DK), F32),
                        pltpu.VMEM((HG_HEADS, ts, HG_DK), F32), pltpu.VMEM((HG_HEADS, ts, HG_DV), F32),
                        pltpu.VMEM((HG_HEADS, ts, HG_DK), BF16), pltpu.VMEM((HG_HEADS, ts, HG_DK), BF16),
                        pltpu.VMEM((HG_HEADS, ts // HG_CHUNK, HG_DK), F32),
                        pltpu.VMEM((HG_HEADS, ts, HG_DV), F32),
                        pltpu.VMEM((HG_HEADS, HG_DV, HG_DK), F32)],
        compiler_params=_cparams(("parallel", "arbitrary")),
        name="hgrn2",
    )(proj, proj, proj, proj, lb.reshape(1, HG_QK), out_norm.reshape(1, HG_DV), tmat)


def _t5_band_buckets():
    t = np.arange(WINDOW)[:, None]
    s = np.arange(2 * WINDOW)[None, :]
    dist = t + WINDOW - s
    max_exact = N_BUCKETS // 2
    large = max_exact + (np.log(np.maximum(dist, max_exact) / max_exact) / math.log(MAX_DISTANCE / max_exact)
                         * (N_BUCKETS - max_exact)).astype(np.int32)
    large = np.minimum(large, N_BUCKETS - 1)
    band = (dist >= 0) & (dist < WINDOW)
    return np.where(dist < max_exact, np.maximum(dist, 0), large).astype(np.int32), band


def _swa_kernel(sink_ref, q_ref, kc_ref, kp_ref, vc_ref, vp_ref, bias_ref, qg_ref, kg_ref, gmat_ref, o_ref):
    n = pl.program_id(1)
    w2 = 2 * WINDOW
    no_prev = jnp.where(n > 0, 0.0, -jnp.inf)
    q_groups = SW_HEADS * SW_DIM // LANES
    kv_groups = SW_KV_HEADS * SW_DIM // LANES
    per_kv = q_groups // kv_groups
    grp = lambda ref, p: ref[0, :, p * LANES:(p + 1) * LANES]

    def head_norm(x, gain):
        sq = jnp.concatenate(_split_bf16(x * x), axis=1)
        ssq = _dot(sq, gmat_ref[...])
        return x * lax.rsqrt(ssq * (1.0 / SW_DIM) + EPS) * gain

    qn = head_norm(jnp.concatenate([grp(q_ref, p) for p in range(q_groups)], axis=0), qg_ref[...])
    kn = head_norm(jnp.concatenate([x for r in range(kv_groups) for x in (grp(kp_ref, r), grp(kc_ref, r))], axis=0),
                   kg_ref[...])
    v = jnp.concatenate([x for r in range(kv_groups) for x in (grp(vp_ref, r), grp(vc_ref, r))], axis=0)
    kn_rot, v_rot = pltpu.roll(kn, SW_DIM, 1), pltpu.roll(v, SW_DIM, 1)
    kn, kn_rot, v, v_rot = (a.astype(BF16) for a in (kn, kn_rot, v, v_rot))
    ones = jnp.ones((w2, LANES), BF16)

    rows_q = per_kv * WINDOW
    lane = lax.broadcasted_iota(jnp.int32, (rows_q, LANES), 1)
    row = lax.broadcasted_iota(jnp.int32, (rows_q, LANES), 0)
    same_half = (lane < SW_DIM) == (row < rows_q // 2)

    logits = []
    for r in range(kv_groups):
        qr = qn[r * rows_q:(r + 1) * rows_q]
        keys = slice(r * w2, (r + 1) * w2)
        zero = jnp.zeros_like(qr)
        logits.append((_dot_nt(jnp.where(same_half, qr, zero).astype(BF16), kn[keys]),
                       _dot_nt(jnp.where(same_half, zero, qr).astype(BF16), kn_rot[keys])))

    weights, sink_terms = [], []
    for r in range(kv_groups):
        w_r, s_r = ([], []), ([], [])
        for j in range(per_kv):
            p = r * per_kv + j
            second = j >= per_kv // 2
            for variant, h in ((0, 2 * p + int(second)), (1, 2 * p + int(not second))):
                lg = logits[r][variant][j * WINDOW:(j + 1) * WINDOW]
                lp = lg[:, :WINDOW] + (bias_ref[h, :, :WINDOW] + no_prev)
                lc = lg[:, WINDOW:] + bias_ref[h, :, WINDOW:]
                sink = sink_ref[h]
                m = jnp.maximum(jnp.max(jnp.maximum(lp, lc), axis=-1, keepdims=True), sink)
                w_r[variant].append(jnp.concatenate([jnp.exp(lp - m), jnp.exp(lc - m)], axis=1).astype(BF16))
                s_r[variant].append(jnp.broadcast_to(jnp.exp(sink - m), (WINDOW, LANES)))
        weights.append(tuple(jnp.concatenate(x, axis=0) for x in w_r))
        sink_terms.append(tuple(jnp.concatenate(x, axis=0) for x in s_r))

    outs = []
    for r in range(kv_groups):
        keys = slice(r * w2, (r + 1) * w2)
        outs.append((_dot(weights[r][0], jnp.concatenate([v[keys], ones], axis=1)),
                     _dot(weights[r][1], jnp.concatenate([v_rot[keys], ones], axis=1))))
    for r in range(kv_groups):
        o = [outs[r][i][:, :LANES] / (outs[r][i][:, LANES:] + sink_terms[r][i]) for i in range(2)]
        o = jnp.where(same_half, o[0], o[1]).astype(o_ref.dtype)
        for j in range(per_kv):
            p = r * per_kv + j
            o_ref[0, :, p * LANES:(p + 1) * LANES] = o[j * WINDOW:(j + 1) * WINDOW]


def _swa_group_sum():
    i = np.arange(2 * LANES)[:, None] % LANES
    j = np.arange(LANES)[None, :]
    return (i // SW_DIM == j // SW_DIM).astype(np.float32)


def sliding_window(proj, q_norm, k_norm, sinks, rel_bias):
    b, s, _ = proj.shape
    bucket, band = _t5_band_buckets()
    onehot = jnp.asarray(bucket[..., None] == np.arange(N_BUCKETS), F32)
    bias = jnp.einsum('tsb,bh->hts', onehot, rel_bias.astype(F32), precision=lax.Precision.HIGHEST)
    bias = jnp.where(band, bias, -jnp.inf)
    kv_w = SW_KV_HEADS * SW_DIM
    q_w = SW_HEADS * SW_DIM
    k_blk = q_w // kv_w
    per_lane_group = LANES // SW_DIM
    q_gain = jnp.tile(q_norm.astype(F32), per_lane_group).reshape(1, LANES) * (SW_DIM ** -0.5)
    k_gain = jnp.tile(k_norm.astype(F32), per_lane_group).reshape(1, LANES)
    prev = lambda n: jnp.maximum(n - 1, 0)
    const2 = lambda i, n: (0, 0)
    return pl.pallas_call(
        _swa_kernel,
        grid=(b, s // WINDOW),
        in_specs=[pl.BlockSpec(memory_space=pltpu.SMEM),
                  pl.BlockSpec((1, WINDOW, q_w), lambda i, n: (i, n, 0)),
                  pl.BlockSpec((1, WINDOW, kv_w), lambda i, n: (i, n, k_blk)),
                  pl.BlockSpec((1, WINDOW, kv_w), lambda i, n: (i, prev(n), k_blk)),
                  pl.BlockSpec((1, WINDOW, kv_w), lambda i, n: (i, n, k_blk + 1)),
                  pl.BlockSpec((1, WINDOW, kv_w), lambda i, n: (i, prev(n), k_blk + 1)),
                  pl.BlockSpec((SW_HEADS, WINDOW, 2 * WINDOW), lambda i, n: (0, 0, 0)),
                  pl.BlockSpec((1, LANES), const2),
                  pl.BlockSpec((1, LANES), const2),
                  pl.BlockSpec((2 * LANES, LANES), const2)],
        out_specs=pl.BlockSpec((1, WINDOW, q_w), lambda i, n: (i, n, 0)),
        out_shape=jax.ShapeDtypeStruct((b, s, q_w), BF16),
        compiler_params=_cparams(("parallel", "parallel")),
        name="sliding_window",
    )(sinks.astype(F32), proj, proj, proj, proj, proj, bias, q_gain, k_gain,
      jnp.asarray(_swa_group_sum(), BF16))


def kernel(x, p, mix_norm, ab_w_in, hg_lb_logits, hg_out_norm, ab_w_out, c_w_in, q_norm, k_norm, sinks,
           rel_bias, c_w_out, ffn_norm, ffn_up, ffn_conv, ffn_conv_b, ffn_down, ple_norm, ple_gate, ple_proj):
    b, s, d = x.shape
    depth = p.shape[0]
    m = b * s
    lb_cum = jnp.cumsum(jax.nn.softmax(hg_lb_logits.astype(F32), axis=0), axis=0)
    lower_bounds = lb_cum - lb_cum[0]
    ab_w_in, ab_w_out, c_w_in, c_w_out, ffn_up, ffn_down, ple_gate, ple_proj = (
        w.astype(BF16) for w in (ab_w_in, ab_w_out, c_w_in, c_w_out, ffn_up, ffn_down, ple_gate, ple_proj))
    conv_b = ffn_conv_b.reshape(depth, 1, -1)
    p = p.reshape(depth, m, -1)

    def in_proj(i):
        if i % 2 == 0:
            return mix_norm[i], ab_w_in, i // 2, (3 * SB_WIDTH, 2 * HG_QK + 2 * HG_V), (BF16, F32)
        return mix_norm[i], c_w_in, i // 2, (c_w_in.shape[2],), (F32,)

    h = x.reshape(m, d)
    proj = norm_proj(h, *in_proj(0))
    for i in range(depth):
        j = i // 2
        if i % 2 == 0:
            att, gates = proj
            o_a = stick_breaking(att.reshape(b, s, -1))
            o_b = hgrn2(gates.reshape(b, s, -1), lower_bounds[j], hg_out_norm[j])
            parts, w_out = [o_a.reshape(m, -1), o_b.reshape(m, -1)], ab_w_out
        else:
            (qkv,) = proj
            o = sliding_window(qkv.reshape(b, s, -1), q_norm[j], k_norm[j], sinks[j], rel_bias)
            parts, w_out = [o.reshape(m, -1)], c_w_out
        h = mix_ffn(parts, w_out, j, h, ffn_norm[i], ffn_up, ffn_conv, conv_b, ffn_down, i, s)
        h, *proj = ple_and_proj(h, ple_norm[i], ple_gate, p, ple_proj, i, in_proj(i + 1) if i + 1 < depth else None)
    return h.reshape(b, s, d)
```

```python
import functools
import math

import numpy as np
import jax
import jax.numpy as jnp
from jax import lax
from jax.experimental import pallas as pl
from jax.experimental.pallas import tpu as pltpu

F32 = jnp.float32
BF16 = jnp.bfloat16

D_MODEL = 1024
PLE_DIM = 256
EPS = 1e-6
SB_HEADS = 8
SB_DIM = 64
SB_WIDTH = SB_HEADS * SB_DIM
HG_HEADS = 4
HG_DK = 128
HG_DV = 128
HG_QK = HG_HEADS * HG_DK
HG_V = HG_HEADS * HG_DV
SW_HEADS = 16
SW_KV_HEADS = 4
SW_DIM = 64
SW_GROUP = SW_HEADS // SW_KV_HEADS
WINDOW = 128
N_BUCKETS = 32
MAX_DISTANCE = 128
D_FF = 2816
CONV_W = 3

LANES = 128
SUBLANES = 8
BLK = 128
HG_CHUNK = 16
HG_BATCH = 2
FF_CHUNK = 256
SB_DEAD_LOG2 = -150.0
SB_NEAR_ROWS = 32
LOG2E = 1.4426950408889634
VMEM_LIMIT = 56 * 1024 * 1024


def _cparams(sem):
    return pltpu.CompilerParams(dimension_semantics=sem, vmem_limit_bytes=VMEM_LIMIT)


def _rms(x, g):
    ms = jnp.mean(x * x, axis=-1, keepdims=True)
    return x * lax.rsqrt(ms + EPS) * g


def _dot(a, b):
    return jnp.dot(a, b, preferred_element_type=F32)


def _dot_nt(a, b):
    return lax.dot_general(a, b, (((1,), (1,)), ((), ())), preferred_element_type=F32)


def _dot_tn(a, b):
    return lax.dot_general(a, b, (((0,), (0,)), ((), ())), preferred_element_type=F32)


def _split_bf16(x):
    hi = x.astype(BF16)
    lo = (x - hi.astype(F32)).astype(BF16)
    return hi, lo


def _layer_spec(stack, layer, single_buffer=False):
    zeros = (0,) * (stack.ndim - 1)
    kwargs = dict(pipeline_mode=pl.Buffered(1)) if single_buffer else {}
    return pl.BlockSpec((None,) + stack.shape[1:], lambda *_: (layer,) + zeros, **kwargs)


def _row_spec(tm, width):
    return pl.BlockSpec((tm, width), lambda i: (i, 0))


def _project(hn, w_ref, o_refs, splits, nchunk):
    col = 0
    for o_ref, width in zip(o_refs, splits):
        for c0 in range(0, width, nchunk):
            cw = min(nchunk, width - c0)
            o_ref[:, c0:c0 + cw] = _dot(hn, w_ref[:, col + c0:col + c0 + cw]).astype(o_ref.dtype)
        col += width


def _norm_proj_kernel(h_ref, g_ref, w_ref, *o_refs, splits, nchunk):
    _project(_rms(h_ref[...], g_ref[...]).astype(BF16), w_ref, o_refs, splits, nchunk)


def norm_proj(h, g, w_stack, layer, splits, dtypes, tm=512, nchunk=512):
    m, k = h.shape
    assert sum(splits) == w_stack.shape[2] and m % tm == 0
    return pl.pallas_call(
        functools.partial(_norm_proj_kernel, splits=tuple(splits), nchunk=nchunk),
        grid=(m // tm,),
        in_specs=[_row_spec(tm, k), pl.BlockSpec((1, k), lambda i: (0, 0)), _layer_spec(w_stack, layer)],
        out_specs=[_row_spec(tm, s) for s in splits],
        out_shape=[jax.ShapeDtypeStruct((m, s), d) for s, d in zip(splits, dtypes)],
        compiler_params=_cparams(("parallel",)),
        name="norm_proj",
    )(h, g.reshape(1, k), w_stack)


def _mix_ffn_kernel(*refs, n_parts, tiles_per_seq):
    a_refs = refs[:n_parts]
    wo_ref, h_ref, g_ref, wup_ref, cw_ref, cb_ref, wdn_ref, o_ref, hn_ref, act_ref, halo_ref = refs[n_parts:]
    tm = h_ref.shape[0]
    x = h_ref[...]
    k0 = 0
    for a_ref in a_refs:
        k = a_ref.shape[1]
        x = x + _dot(a_ref[...], wo_ref[k0:k0 + k, :])
        k0 += k
    hn_ref[...] = _rms(x, g_ref[...]).astype(BF16)
    seq_start = (pl.program_id(0) % tiles_per_seq) == 0
    row = lax.broadcasted_iota(jnp.int32, (tm, FF_CHUNK), 0)

    def conv_cols(c0):
        cols = slice(c0, c0 + FF_CHUNK)
        u = _dot(hn_ref[...], wup_ref[:, cols])
        prev = jnp.where(seq_start, 0.0, halo_ref[:, cols])
        halo_ref[:, cols] = u[tm - SUBLANES:, :]
        p1 = prev[SUBLANES - 1:SUBLANES, :]
        p2 = prev[SUBLANES - 2:SUBLANES - 1, :]
        u1 = jnp.where(row == 0, p1, pltpu.roll(u, 1, 0))
        u2 = jnp.where(row == 0, p2, jnp.where(row == 1, p1, pltpu.roll(u, 2, 0)))
        cw = cw_ref[:, cols]
        return cw[0:1, :] * u2 + cw[1:2, :] * u1 + cw[2:3, :] * u + cb_ref[:, cols]

    for c in range(D_FF // FF_CHUNK):
        gate = conv_cols(c * FF_CHUNK)
        up = conv_cols(D_FF + c * FF_CHUNK)
        act_ref[:, c * FF_CHUNK:(c + 1) * FF_CHUNK] = (gate * jax.nn.sigmoid(gate) * up).astype(BF16)
    o_ref[...] = x + _dot(act_ref[...], wdn_ref[...])


def mix_ffn(parts, w_out, out_layer, h, g, w_up, conv_w, conv_b, w_down, layer, seq, tm=1024):
    m, d = h.shape
    f2 = w_up.shape[2]
    assert seq % tm == 0 and D_FF % FF_CHUNK == 0 and sum(a.shape[1] for a in parts) == w_out.shape[1]
    return pl.pallas_call(
        functools.partial(_mix_ffn_kernel, n_parts=len(parts), tiles_per_seq=seq // tm),
        grid=(m // tm,),
        in_specs=[_row_spec(tm, a.shape[1]) for a in parts]
                 + [_layer_spec(w_out, out_layer, True), _row_spec(tm, d), pl.BlockSpec((1, d), lambda i: (0, 0)),
                    _layer_spec(w_up, layer, True), _layer_spec(conv_w, layer), _layer_spec(conv_b, layer),
                    _layer_spec(w_down, layer, True)],
        out_specs=_row_spec(tm, d),
        out_shape=jax.ShapeDtypeStruct((m, d), F32),
        scratch_shapes=[pltpu.VMEM((tm, d), BF16),
                        pltpu.VMEM((tm, D_FF), BF16),
                        pltpu.VMEM((SUBLANES, f2), F32)],
        compiler_params=_cparams(("arbitrary",)),
        name="mix_ffn",
    )(*parts, w_out, h, g.reshape(1, d), w_up, conv_w, conv_b, w_down)


def _ple_proj_kernel(*refs, splits, nchunk):
    h_ref, g_ref, wg_ref, p_ref, wp_ref = refs[:5]
    x = h_ref[...]
    gate = jax.nn.sigmoid(_dot(_rms(x, g_ref[...]).astype(BF16), wg_ref[...]))
    x = x + gate * _dot(p_ref[...].astype(BF16), wp_ref[...])
    if splits:
        g2_ref, win_ref, o_ref, *proj_refs = refs[5:]
        _project(_rms(x, g2_ref[...]).astype(BF16), win_ref, proj_refs, splits, nchunk)
    else:
        (o_ref,) = refs[5:]
    o_ref[...] = x


def ple_and_proj(h, g, w_gate, p, w_proj, layer, nxt=None, tm=512, nchunk=512):
    m, d = h.shape
    in_specs = [_row_spec(tm, d), pl.BlockSpec((1, d), lambda i: (0, 0)), _layer_spec(w_gate, layer, True),
                pl.BlockSpec((None, tm, p.shape[2]), lambda i: (layer, i, 0)), _layer_spec(w_proj, layer, True)]
    args = [h, g.reshape(1, d), w_gate, p, w_proj]
    out_specs = [_row_spec(tm, d)]
    out_shape = [jax.ShapeDtypeStruct((m, d), F32)]
    splits = ()
    if nxt is not None:
        g2, w_in, in_layer, splits, dtypes = nxt
        assert sum(splits) == w_in.shape[2]
        in_specs += [pl.BlockSpec((1, d), lambda i: (0, 0)), _layer_spec(w_in, in_layer, True)]
        args += [g2.reshape(1, d), w_in]
        out_specs += [_row_spec(tm, s) for s in splits]
        out_shape += [jax.ShapeDtypeStruct((m, s), dt) for s, dt in zip(splits, dtypes)]
    return pl.pallas_call(
        functools.partial(_ple_proj_kernel, splits=tuple(splits), nchunk=nchunk),
        grid=(m // tm,),
        in_specs=in_specs,
        out_specs=out_specs,
        out_shape=out_shape,
        compiler_params=_cparams(("parallel",)),
        name="ple_proj",
    )(*args)


def _sb_tri():
    j = np.arange(2 * BLK)[:, None] % BLK
    s = np.arange(2 * BLK)[None, :]
    return ((s >= BLK) | (j >= s)).astype(np.float32)


def _neg_abs(x):
    sign = jnp.uint32(0x80000000)
    return lax.bitcast_convert_type(lax.bitcast_convert_type(x, jnp.uint32) | sign, F32)


def _sb_kernel(q_ref, k_ref, v_ref, tri_ref, o_ref, qs_ref, acc_ref, c_ref):
    n = pl.program_id(1)
    npair = SB_WIDTH // LANES
    pairs = range(npair)
    cols = [slice(p * LANES, (p + 1) * LANES) for p in pairs]
    lane = lax.broadcasted_iota(jnp.int32, (2 * BLK, BLK), 1)
    row = lax.broadcasted_iota(jnp.int32, (2 * BLK, BLK), 0)
    first_head = row < BLK
    causal = lane < jnp.where(first_head, row, row - BLK)
    own_lanes = first_head == (lane < SB_DIM)

    for p in pairs:
        q2 = q_ref[0, :, cols[p]] * (SB_DIM ** -0.5)
        q2 = jnp.concatenate([q2, q2], axis=0)
        qs_ref[p] = jnp.where(own_lanes, q2, jnp.zeros_like(q2))

    def stacked(ref, p, lo, hi):
        if (lo, hi) == (0, BLK):
            return ref[p]
        return jnp.concatenate([ref[p, lo:hi], ref[p, BLK + lo:BLK + hi]], axis=0)

    def log_keep_sums(t, mask):
        log_keep = jnp.minimum(t, 0.0) - jnp.log2(1.0 + jnp.exp2(_neg_abs(t)))
        if mask:
            log_keep = jnp.where(causal, log_keep, 0.0)
        return jnp.concatenate(_split_bf16(log_keep), axis=1)

    def process(kb, nblk, diagonal, lo=0, hi=BLK, near_kb=None):
        r, rn = hi - lo, SB_NEAR_ROWS
        near = near_kb is not None
        assert not near or (lo, hi) == (0, BLK)
        krows = pl.ds(pl.multiple_of(kb * BLK, BLK), nblk * BLK)
        first_half = lax.broadcasted_iota(jnp.int32, (r, BLK), 1) < SB_DIM
        tn = [_dot_nt(stacked(qs_ref, p, lo, hi), k_ref[0, krows, cols[p]]) * (-LOG2E) for p in pairs]
        if near:
            nrows = pl.ds(pl.multiple_of(near_kb * BLK, BLK), BLK)
            tn_n = [_dot_nt(stacked(qs_ref, p, 0, rn), k_ref[0, nrows, cols[p]]) * (-LOG2E) for p in pairs]
        order = list(range(nblk - 1, -1, -1))
        split = {(p, j): log_keep_sums(tn[p][:, j * BLK:(j + 1) * BLK], diagonal and j == nblk - 1)
                 for p in pairs for j in order}
        if near:
            split_n = [log_keep_sums(tn_n[p], False) for p in pairs]
        sums = {(p, j): _dot(split[p, j], tri_ref[...]) for p in pairs for j in order}
        if near:
            sums_n = [_dot(split_n[p], tri_ref[...]) for p in pairs]
        ws, ws_n = [], []
        for p in pairs:
            c = None if diagonal else stacked(c_ref, p, lo, hi)
            w = [None] * nblk
            for j in order:
                incl, total = sums[p, j][:, :BLK], sums[p, j][:, BLK:]
                arg = incl - tn[p][:, j * BLK:(j + 1) * BLK]
                if c is not None:
                    arg = arg + c
                wj = jnp.exp2(arg)
                if diagonal and j == nblk - 1:
                    wj = jnp.where(causal, wj, 0.0)
                w[j] = wj.astype(BF16)
                c = total if c is None else c + total
            ws.append(w[0] if nblk == 1 else jnp.concatenate(w, axis=1))
            if (lo, hi) == (0, BLK):
                c_ref[p] = c
            else:
                c_ref[p, lo:hi] = c[:r]
                c_ref[p, BLK + lo:BLK + hi] = c[r:]
            if near:
                c_n = jnp.concatenate([c[:rn], c[BLK:BLK + rn]], axis=0)
                ws_n.append(jnp.exp2(sums_n[p][:, :BLK] - tn_n[p] + c_n).astype(BF16))
                c_n = c_n + sums_n[p][:, BLK:]
                c_ref[p, 0:rn] = c_n[:rn]
                c_ref[p, BLK:BLK + rn] = c_n[rn:]
        pvs = [_dot(ws[p], v_ref[0, krows, cols[p]]) for p in pairs]
        if near:
            pvs_n = [_dot(ws_n[p], v_ref[0, nrows, cols[p]]) for p in pairs]
        for p in pairs:
            pv = jnp.where(first_half, pvs[p][:r], pvs[p][r:])
            if near:
                pv_n = jnp.where(lax.broadcasted_iota(jnp.int32, (rn, BLK), 1) < SB_DIM, pvs_n[p][:rn], pvs_n[p][rn:])
                pv = jnp.concatenate([pv[:rn] + pv_n, pv[rn:]], axis=0)
            if diagonal:
                acc_ref[p] = pv
            else:
                acc_ref[p, lo:hi] += pv

    def live(lo, hi):
        c = c_ref[...]
        return jnp.max(jnp.maximum(c[:, lo:hi], c[:, BLK + lo:BLK + hi])) > SB_DEAD_LOG2

    @pl.when(n == 0)
    def _():
        process(n, 1, True)

    @pl.when(n == 1)
    def _():
        process(n - 1, 2, True)

    @pl.when(n >= 2)
    def _():
        process(n - 1, 2, True, near_kb=n - 2)

        @pl.when(live(SB_NEAR_ROWS, BLK))
        def _():
            process(n - 2, 1, False, SB_NEAR_ROWS, BLK)

    def full_block(state):
        process(state[0], 1, False)
        return state[0] - 1, live(SB_NEAR_ROWS, BLK)

    def near_block(state):
        process(state[0], 1, False, 0, SB_NEAR_ROWS)
        return state[0] - 1, live(0, SB_NEAR_ROWS)

    more = lambda st: (st[0] >= 0) & st[1]
    kb, _ = lax.while_loop(more, full_block, (n - 3, live(SB_NEAR_ROWS, BLK)))
    lax.while_loop(more, near_block, (kb, live(0, SB_NEAR_ROWS)))
    for p in pairs:
        o_ref[0, :, cols[p]] = acc_ref[p].astype(o_ref.dtype)


def stick_breaking(proj):
    b, s, _ = proj.shape
    tri = jnp.asarray(_sb_tri(), BF16)
    npair = SB_WIDTH // LANES
    return pl.pallas_call(
        _sb_kernel,
        grid=(b, s // BLK),
        in_specs=[pl.BlockSpec((1, BLK, SB_WIDTH), lambda i, n: (i, n, 0)),
                  pl.BlockSpec((1, s, SB_WIDTH), lambda i, n: (i, 0, 1)),
                  pl.BlockSpec((1, s, SB_WIDTH), lambda i, n: (i, 0, 2)),
                  pl.BlockSpec((2 * BLK, 2 * BLK), lambda i, n: (0, 0))],
        out_specs=pl.BlockSpec((1, BLK, SB_WIDTH), lambda i, n: (i, n, 0)),
        out_shape=jax.ShapeDtypeStruct((b, s, SB_WIDTH), BF16),
        scratch_shapes=[pltpu.VMEM((npair, 2 * BLK, LANES), BF16),
                        pltpu.VMEM((npair, BLK, LANES), F32),
                        pltpu.VMEM((npair, 2 * BLK, LANES), F32)],
        compiler_params=_cparams(("parallel", "arbitrary")),
        name="stick_breaking",
    )(proj, proj, proj, tri)


def _hg_time_mats():
    t = np.arange(BLK)[:, None]
    j = np.arange(BLK)[None, :]
    same = (t // HG_CHUNK) == (j // HG_CHUNK)
    per_chunk = np.arange(2 * SUBLANES)[:, None] == (j // HG_CHUNK)
    return np.concatenate([same & (j <= t), same, per_chunk], axis=0).astype(np.float32)


def _hgrn_kernel(q_ref, f_ref, i_ref, g_ref, lb_ref, gn_ref, tmat_ref, o_ref,
                 qf_s, kk_s, b_s, iv_s, qt_s, kt_s, dec_s, raw_s, state_s):
    ts = q_ref.shape[1]
    units = [(bi, h) for bi in range(q_ref.shape[0]) for h in range(HG_HEADS)]
    hcols = [slice(h * LANES, (h + 1) * LANES) for h in range(HG_HEADS)]
    half = HG_CHUNK // 2
    cpb = BLK // HG_CHUNK

    @pl.when(pl.program_id(1) == 0)
    def _():
        state_s[...] = jnp.zeros_like(state_s)

    def prep(r, _):
        rows = pl.ds(pl.multiple_of(r * BLK, BLK), BLK)
        for u, (bi, h) in enumerate(units):
            lb = lb_ref[:, hcols[h]]
            fp = f_ref[bi, rows, hcols[h]]
            q = q_ref[bi, rows, hcols[h]]
            e = jnp.exp(-jnp.abs(fp))
            sg = 1.0 / (1.0 + e)
            log_f = jnp.log(lb + (1.0 - lb) * jnp.where(fp >= 0, sg, e * sg))
            kk = (1.0 - lb) * jnp.where(fp >= 0, e * sg, sg)
            qf = q * jax.nn.sigmoid(q)
            hi, lo = _split_bf16(log_f)
            sums = _dot(tmat_ref[...], jnp.concatenate([hi, lo], axis=1))
            sums = sums[:, :LANES] + sums[:, LANES:]
            b = sums[:BLK]
            b_tot = sums[BLK:2 * BLK]
            chunk_tot = sums[2 * BLK:2 * BLK + cpb]
            qf_s[u, rows, :] = qf
            kk_s[u, rows, :] = kk
            b_s[u, rows, :] = b
            iv_s[u, rows, :] = i_ref[bi, rows, hcols[h]]
            qt_s[u, rows, :] = (qf * jnp.exp(b)).astype(BF16)
            kt_s[u, rows, :] = (kk * jnp.exp(b_tot - b)).astype(BF16)
            dec_s[u, pl.ds(pl.multiple_of(r * cpb, cpb), cpb), :] = jnp.exp(chunk_tot)
        return 0

    lax.fori_loop(0, ts // BLK, prep, 0)

    rowi = lax.broadcasted_iota(jnp.int32, (half, 1), 0)

    def chunk(c, _):
        r0 = pl.multiple_of(c * HG_CHUNK, HG_CHUNK)
        rows = pl.ds(r0, HG_CHUNK)
        lo_rows = pl.ds(r0, half)
        hi_rows = pl.ds(r0 + half, half)
        inter = [_dot_nt(qt_s[u, rows, :], state_s[u].astype(BF16)) for u in range(len(units))]
        for u in range(len(units)):
            qf_lo, qf_hi = qf_s[u, lo_rows, :], qf_s[u, hi_rows, :]
            b_lo, b_hi = b_s[u, lo_rows, :], b_s[u, hi_rows, :]
            o_lo = o_hi = jnp.zeros((half, LANES), F32)
            for s in range(HG_CHUNK):
                one = pl.ds(r0 + s, 1)
                b_key, k_key, v_key = b_s[u, one, :], kk_s[u, one, :], iv_s[u, one, :]
                if s < half:
                    score = jnp.sum(qf_lo * k_key * jnp.exp(b_lo - b_key), axis=-1, keepdims=True)
                    if s > 0:
                        score = jnp.where(rowi >= s, score, 0.0)
                    o_lo = o_lo + score * v_key
                score = jnp.sum(qf_hi * k_key * jnp.exp(b_hi - b_key), axis=-1, keepdims=True)
                if s > half:
                    score = jnp.where(rowi >= s - half, score, 0.0)
                o_hi = o_hi + score * v_key
            raw_s[u, lo_rows, :] = o_lo + inter[u][:half]
            raw_s[u, hi_rows, :] = o_hi + inter[u][half:]
        upd = [_dot_tn(iv_s[u, rows, :].astype(BF16), kt_s[u, rows, :]) for u in range(len(units))]
        for u in range(len(units)):
            state_s[u] = state_s[u] * dec_s[u, pl.ds(c, 1), :] + upd[u]
        return 0

    lax.fori_loop(0, ts // HG_CHUNK, chunk, 0)

    def finish(r, _):
        rows = pl.ds(pl.multiple_of(r * BLK, BLK), BLK)
        for u, (bi, h) in enumerate(units):
            g = g_ref[bi, rows, hcols[h]]
            y = _rms(raw_s[u, rows, :], gn_ref[...]) * (g * jax.nn.sigmoid(g))
            o_ref[bi, rows, hcols[h]] = y.astype(o_ref.dtype)
        return 0

    lax.fori_loop(0, ts // BLK, finish, 0)


def hgrn2(proj, lb, out_norm, ts=512):
    b, s, _ = proj.shape
    assert s % ts == 0 and ts % BLK == 0 and b % HG_BATCH == 0
    tmat = jnp.asarray(_hg_time_mats(), BF16)
    seq_blk = lambda part: pl.BlockSpec((HG_BATCH, ts, HG_QK), lambda i, t: (i, t, part))
    const = lambda i, t: (0, 0)
    nu = HG_BATCH * HG_HEADS
    return pl.pallas_call(
        _hgrn_kernel,
        grid=(b // HG_BATCH, s // ts),
        in_specs=[seq_blk(0), seq_blk(1), seq_blk(2), seq_blk(3),
                  pl.BlockSpec((1, HG_QK), const),
                  pl.BlockSpec((1, HG_DV), const),
                  pl.BlockSpec(tmat.shape, const)],
        out_specs=pl.BlockSpec((HG_BATCH, ts, HG_V), lambda i, t: (i, t, 0)),
        out_shape=jax.ShapeDtypeStruct((b, s, HG_V), BF16),
        scratch_shapes=[pltpu.VMEM((nu, ts, HG_DK), F32), pltpu.VMEM((nu, ts, HG_DK), F32),
                        pltpu.VMEM((nu, ts, HG_DK), F32), pltpu.VMEM((nu, ts, HG_DV), F32),
                        pltpu.VMEM((nu, ts, HG_DK), BF16), pltpu.VMEM((nu, ts, HG_DK), BF16),
                        pltpu.VMEM((nu, ts // HG_CHUNK, HG_DK), F32),
                        pltpu.VMEM((nu, ts, HG_DV), F32),
                        pltpu.VMEM((nu, HG_DV, HG_DK), F32)],
        compiler_params=_cparams(("parallel", "arbitrary")),
        name="hgrn2",
    )(proj, proj, proj, proj, lb.reshape(1, HG_QK), out_norm.reshape(1, HG_DV), tmat)


def _t5_band_buckets():
    t = np.arange(WINDOW)[:, None]
    s = np.arange(2 * WINDOW)[None, :]
    dist = t + WINDOW - s
    max_exact = N_BUCKETS // 2
    large = max_exact + (np.log(np.maximum(dist, max_exact) / max_exact) / math.log(MAX_DISTANCE / max_exact)
                         * (N_BUCKETS - max_exact)).astype(np.int32)
    large = np.minimum(large, N_BUCKETS - 1)
    band = (dist >= 0) & (dist < WINDOW)
    return np.where(dist < max_exact, np.maximum(dist, 0), large).astype(np.int32), band


def _swa_kernel(sink_ref, q_ref, kc_ref, kp_ref, vc_ref, vp_ref, bias_ref, qg_ref, kg_ref, gmat_ref, o_ref):
    n = pl.program_id(1)
    w2 = 2 * WINDOW
    no_prev = jnp.where(n > 0, 0.0, -jnp.inf)
    q_groups = SW_HEADS * SW_DIM // LANES
    kv_groups = SW_KV_HEADS * SW_DIM // LANES
    per_kv = q_groups // kv_groups
    grp = lambda ref, p: ref[0, :, p * LANES:(p + 1) * LANES]

    def head_norm(x, gain):
        sq = jnp.concatenate(_split_bf16(x * x), axis=1)
        ssq = _dot(sq, gmat_ref[...])
        return x * lax.rsqrt(ssq * (1.0 / SW_DIM) + EPS) * gain

    qn = head_norm(jnp.concatenate([grp(q_ref, p) for p in range(q_groups)], axis=0), qg_ref[...])
    kn = head_norm(jnp.concatenate([x for r in range(kv_groups) for x in (grp(kp_ref, r), grp(kc_ref, r))], axis=0),
                   kg_ref[...])
    v = jnp.concatenate([x for r in range(kv_groups) for x in (grp(vp_ref, r), grp(vc_ref, r))], axis=0)
    kn_rot, v_rot = pltpu.roll(kn, SW_DIM, 1), pltpu.roll(v, SW_DIM, 1)
    kn, kn_rot, v, v_rot = (a.astype(BF16) for a in (kn, kn_rot, v, v_rot))
    ones = jnp.ones((w2, LANES), BF16)

    rows_q = per_kv * WINDOW
    lane = lax.broadcasted_iota(jnp.int32, (rows_q, LANES), 1)
    row = lax.broadcasted_iota(jnp.int32, (rows_q, LANES), 0)
    same_half = (lane < SW_DIM) == (row < rows_q // 2)

    logits = []
    for r in range(kv_groups):
        qr = qn[r * rows_q:(r + 1) * rows_q]
        keys = slice(r * w2, (r + 1) * w2)
        zero = jnp.zeros_like(qr)
        logits.append((_dot_nt(jnp.where(same_half, qr, zero).astype(BF16), kn[keys]),
                       _dot_nt(jnp.where(same_half, zero, qr).astype(BF16), kn_rot[keys])))

    weights, sink_terms = [], []
    for r in range(kv_groups):
        w_r, s_r = ([], []), ([], [])
        for j in range(per_kv):
            p = r * per_kv + j
            second = j >= per_kv // 2
            for variant, h in ((0, 2 * p + int(second)), (1, 2 * p + int(not second))):
                lg = logits[r][variant][j * WINDOW:(j + 1) * WINDOW]
                lp = lg[:, :WINDOW] + (bias_ref[h, :, :WINDOW] + no_prev)
                lc = lg[:, WINDOW:] + bias_ref[h, :, WINDOW:]
                sink = sink_ref[h]
                m = jnp.maximum(jnp.max(jnp.maximum(lp, lc), axis=-1, keepdims=True), sink)
                w_r[variant].append(jnp.concatenate([jnp.exp(lp - m), jnp.exp(lc - m)], axis=1).astype(BF16))
                s_r[variant].append(jnp.broadcast_to(jnp.exp(sink - m), (WINDOW, LANES)))
        weights.append(tuple(jnp.concatenate(x, axis=0) for x in w_r))
        sink_terms.append(tuple(jnp.concatenate(x, axis=0) for x in s_r))

    outs = []
    for r in range(kv_groups):
        keys = slice(r * w2, (r + 1) * w2)
        outs.append((_dot(weights[r][0], jnp.concatenate([v[keys], ones], axis=1)),
                     _dot(weights[r][1], jnp.concatenate([v_rot[keys], ones], axis=1))))
    for r in range(kv_groups):
        o = [outs[r][i][:, :LANES] / (outs[r][i][:, LANES:] + sink_terms[r][i]) for i in range(2)]
        o = jnp.where(same_half, o[0], o[1]).astype(o_ref.dtype)
        for j in range(per_kv):
            p = r * per_kv + j
            o_ref[0, :, p * LANES:(p + 1) * LANES] = o[j * WINDOW:(j + 1) * WINDOW]


def _swa_group_sum():
    i = np.arange(2 * LANES)[:, None] % LANES
    j = np.arange(LANES)[None, :]
    return (i // SW_DIM == j // SW_DIM).astype(np.float32)


def sliding_window(proj, q_norm, k_norm, sinks, rel_bias):
    b, s, _ = proj.shape
    bucket, band = _t5_band_buckets()
    onehot = jnp.asarray(bucket[..., None] == np.arange(N_BUCKETS), F32)
    bias = jnp.einsum('tsb,bh->hts', onehot, rel_bias.astype(F32), precision=lax.Precision.HIGHEST)
    bias = jnp.where(band, bias, -jnp.inf)
    kv_w = SW_KV_HEADS * SW_DIM
    q_w = SW_HEADS * SW_DIM
    k_blk = q_w // kv_w
    per_lane_group = LANES // SW_DIM
    q_gain = jnp.tile(q_norm.astype(F32), per_lane_group).reshape(1, LANES) * (SW_DIM ** -0.5)
    k_gain = jnp.tile(k_norm.astype(F32), per_lane_group).reshape(1, LANES)
    prev = lambda n: jnp.maximum(n - 1, 0)
    const2 = lambda i, n: (0, 0)
    return pl.pallas_call(
        _swa_kernel,
        grid=(b, s // WINDOW),
        in_specs=[pl.BlockSpec(memory_space=pltpu.SMEM),
                  pl.BlockSpec((1, WINDOW, q_w), lambda i, n: (i, n, 0)),
                  pl.BlockSpec((1, WINDOW, kv_w), lambda i, n: (i, n, k_blk)),
                  pl.BlockSpec((1, WINDOW, kv_w), lambda i, n: (i, prev(n), k_blk)),
                  pl.BlockSpec((1, WINDOW, kv_w), lambda i, n: (i, n, k_blk + 1)),
                  pl.BlockSpec((1, WINDOW, kv_w), lambda i, n: (i, prev(n), k_blk + 1)),
                  pl.BlockSpec((SW_HEADS, WINDOW, 2 * WINDOW), lambda i, n: (0, 0, 0)),
                  pl.BlockSpec((1, LANES), const2),
                  pl.BlockSpec((1, LANES), const2),
                  pl.BlockSpec((2 * LANES, LANES), const2)],
        out_specs=pl.BlockSpec((1, WINDOW, q_w), lambda i, n: (i, n, 0)),
        out_shape=jax.ShapeDtypeStruct((b, s, q_w), BF16),
        compiler_params=_cparams(("parallel", "parallel")),
        name="sliding_window",
    )(sinks.astype(F32), proj, proj, proj, proj, proj, bias, q_gain, k_gain,
      jnp.asarray(_swa_group_sum(), BF16))


def kernel(x, p, mix_norm, ab_w_in, hg_lb_logits, hg_out_norm, ab_w_out, c_w_in, q_norm, k_norm, sinks,
           rel_bias, c_w_out, ffn_norm, ffn_up, ffn_conv, ffn_conv_b, ffn_down, ple_norm, ple_gate, ple_proj):
    b, s, d = x.shape
    depth = p.shape[0]
    m = b * s
    lb_cum = jnp.cumsum(jax.nn.softmax(hg_lb_logits.astype(F32), axis=0), axis=0)
    lower_bounds = lb_cum - lb_cum[0]
    ab_w_in, ab_w_out, c_w_in, c_w_out, ffn_up, ffn_down, ple_gate, ple_proj = (
        w.astype(BF16) for w in (ab_w_in, ab_w_out, c_w_in, c_w_out, ffn_up, ffn_down, ple_gate, ple_proj))
    conv_b = ffn_conv_b.reshape(depth, 1, -1)
    p = p.reshape(depth, m, -1)

    def in_proj(i):
        if i % 2 == 0:
            return mix_norm[i], ab_w_in, i // 2, (3 * SB_WIDTH, 2 * HG_QK + 2 * HG_V), (BF16, F32)
        return mix_norm[i], c_w_in, i // 2, (c_w_in.shape[2],), (F32,)

    h = x.reshape(m, d)
    proj = norm_proj(h, *in_proj(0))
    for i in range(depth):
        j = i // 2
        if i % 2 == 0:
            att, gates = proj
            o_a = stick_breaking(att.reshape(b, s, -1))
            o_b = hgrn2(gates.reshape(b, s, -1), lower_bounds[j], hg_out_norm[j])
            parts, w_out = [o_a.reshape(m, -1), o_b.reshape(m, -1)], ab_w_out
        else:
            (qkv,) = proj
            o = sliding_window(qkv.reshape(b, s, -1), q_norm[j], k_norm[j], sinks[j], rel_bias)
            parts, w_out = [o.reshape(m, -1)], c_w_out
        h = mix_ffn(parts, w_out, j, h, ffn_norm[i], ffn_up, ffn_conv, conv_b, ffn_down, i, s)
        h, *proj = ple_and_proj(h, ple_norm[i], ple_gate, p, ple_proj, i, in_proj(i + 1) if i + 1 < depth else None)
    return h.reshape(b, s, d)
```

```python
import functools
import math

import numpy as np
import jax
import jax.numpy as jnp
from jax import lax
from jax.experimental import pallas as pl
from jax.experimental.pallas import tpu as pltpu

F32 = jnp.float32
BF16 = jnp.bfloat16

D_MODEL = 1024
PLE_DIM = 256
EPS = 1e-6
SB_HEADS = 8
SB_DIM = 64
SB_WIDTH = SB_HEADS * SB_DIM
HG_HEADS = 4
HG_DK = 128
HG_DV = 128
HG_QK = HG_HEADS * HG_DK
HG_V = HG_HEADS * HG_DV
SW_HEADS = 16
SW_KV_HEADS = 4
SW_DIM = 64
SW_GROUP = SW_HEADS // SW_KV_HEADS
WINDOW = 128
SW_QBLK = 2
N_BUCKETS = 32
MAX_DISTANCE = 128
D_FF = 2816
CONV_W = 3

LANES = 128
SUBLANES = 8
BLK = 128
HG_CHUNK = 16
HG_BATCH = 2
FF_CHUNK = 256
SB_DEAD_LOG2 = -150.0
SB_NEAR_ROWS = 32
LOG2E = 1.4426950408889634
VMEM_LIMIT = 56 * 1024 * 1024


def _cparams(sem):
    return pltpu.CompilerParams(dimension_semantics=sem, vmem_limit_bytes=VMEM_LIMIT)


def _rms(x, g):
    ms = jnp.mean(x * x, axis=-1, keepdims=True)
    return x * lax.rsqrt(ms + EPS) * g


def _dot(a, b):
    return jnp.dot(a, b, preferred_element_type=F32)


def _dot_nt(a, b):
    return lax.dot_general(a, b, (((1,), (1,)), ((), ())), preferred_element_type=F32)


def _dot_tn(a, b):
    return lax.dot_general(a, b, (((0,), (0,)), ((), ())), preferred_element_type=F32)


def _split_bf16(x):
    hi = x.astype(BF16)
    lo = (x - hi.astype(F32)).astype(BF16)
    return hi, lo


def _layer_spec(stack, layer, single_buffer=False):
    zeros = (0,) * (stack.ndim - 1)
    kwargs = dict(pipeline_mode=pl.Buffered(1)) if single_buffer else {}
    return pl.BlockSpec((None,) + stack.shape[1:], lambda *_: (layer,) + zeros, **kwargs)


def _row_spec(tm, width):
    return pl.BlockSpec((tm, width), lambda i: (i, 0))


def _project(hn, w_ref, o_refs, splits, nchunk):
    col = 0
    for o_ref, width in zip(o_refs, splits):
        for c0 in range(0, width, nchunk):
            cw = min(nchunk, width - c0)
            o_ref[:, c0:c0 + cw] = _dot(hn, w_ref[:, col + c0:col + c0 + cw]).astype(o_ref.dtype)
        col += width


def _norm_proj_kernel(h_ref, g_ref, w_ref, *o_refs, splits, nchunk):
    _project(_rms(h_ref[...], g_ref[...]).astype(BF16), w_ref, o_refs, splits, nchunk)


def norm_proj(h, g, w_stack, layer, splits, dtypes, tm=512, nchunk=512):
    m, k = h.shape
    assert sum(splits) == w_stack.shape[2] and m % tm == 0
    return pl.pallas_call(
        functools.partial(_norm_proj_kernel, splits=tuple(splits), nchunk=nchunk),
        grid=(m // tm,),
        in_specs=[_row_spec(tm, k), pl.BlockSpec((1, k), lambda i: (0, 0)), _layer_spec(w_stack, layer)],
        out_specs=[_row_spec(tm, s) for s in splits],
        out_shape=[jax.ShapeDtypeStruct((m, s), d) for s, d in zip(splits, dtypes)],
        compiler_params=_cparams(("parallel",)),
        name="norm_proj",
    )(h, g.reshape(1, k), w_stack)


def _mix_ffn_kernel(*refs, n_parts, tiles_per_seq):
    a_refs = refs[:n_parts]
    wo_ref, h_ref, g_ref, wup_ref, cw_ref, cb_ref, wdn_ref, o_ref, hn_ref, act_ref, halo_ref = refs[n_parts:]
    tm = h_ref.shape[0]
    x = h_ref[...]
    k0 = 0
    for a_ref in a_refs:
        k = a_ref.shape[1]
        x = x + _dot(a_ref[...], wo_ref[k0:k0 + k, :])
        k0 += k
    hn_ref[...] = _rms(x, g_ref[...]).astype(BF16)
    seq_start = (pl.program_id(0) % tiles_per_seq) == 0
    row = lax.broadcasted_iota(jnp.int32, (tm, FF_CHUNK), 0)

    def conv_cols(c0):
        cols = slice(c0, c0 + FF_CHUNK)
        u = _dot(hn_ref[...], wup_ref[:, cols])
        prev = jnp.where(seq_start, 0.0, halo_ref[:, cols])
        halo_ref[:, cols] = u[tm - SUBLANES:, :]
        p1 = prev[SUBLANES - 1:SUBLANES, :]
        p2 = prev[SUBLANES - 2:SUBLANES - 1, :]
        u1 = jnp.where(row == 0, p1, pltpu.roll(u, 1, 0))
        u2 = jnp.where(row == 0, p2, jnp.where(row == 1, p1, pltpu.roll(u, 2, 0)))
        cw = cw_ref[:, cols]
        return cw[0:1, :] * u2 + cw[1:2, :] * u1 + cw[2:3, :] * u + cb_ref[:, cols]

    for c in range(D_FF // FF_CHUNK):
        gate = conv_cols(c * FF_CHUNK)
        up = conv_cols(D_FF + c * FF_CHUNK)
        act_ref[:, c * FF_CHUNK:(c + 1) * FF_CHUNK] = (gate * jax.nn.sigmoid(gate) * up).astype(BF16)
    o_ref[...] = x + _dot(act_ref[...], wdn_ref[...])


def mix_ffn(parts, w_out, out_layer, h, g, w_up, conv_w, conv_b, w_down, layer, seq, tm=1024):
    m, d = h.shape
    f2 = w_up.shape[2]
    assert seq % tm == 0 and D_FF % FF_CHUNK == 0 and sum(a.shape[1] for a in parts) == w_out.shape[1]
    return pl.pallas_call(
        functools.partial(_mix_ffn_kernel, n_parts=len(parts), tiles_per_seq=seq // tm),
        grid=(m // tm,),
        in_specs=[_row_spec(tm, a.shape[1]) for a in parts]
                 + [_layer_spec(w_out, out_layer, True), _row_spec(tm, d), pl.BlockSpec((1, d), lambda i: (0, 0)),
                    _layer_spec(w_up, layer, True), _layer_spec(conv_w, layer), _layer_spec(conv_b, layer),
                    _layer_spec(w_down, layer, True)],
        out_specs=_row_spec(tm, d),
        out_shape=jax.ShapeDtypeStruct((m, d), F32),
        scratch_shapes=[pltpu.VMEM((tm, d), BF16),
                        pltpu.VMEM((tm, D_FF), BF16),
                        pltpu.VMEM((SUBLANES, f2), F32)],
        compiler_params=_cparams(("arbitrary",)),
        name="mix_ffn",
    )(*parts, w_out, h, g.reshape(1, d), w_up, conv_w, conv_b, w_down)


def _ple_proj_kernel(*refs, splits, nchunk):
    h_ref, g_ref, wg_ref, p_ref, wp_ref = refs[:5]
    x = h_ref[...]
    gate = jax.nn.sigmoid(_dot(_rms(x, g_ref[...]).astype(BF16), wg_ref[...]))
    x = x + gate * _dot(p_ref[...].astype(BF16), wp_ref[...])
    if splits:
        g2_ref, win_ref, o_ref, *proj_refs = refs[5:]
        _project(_rms(x, g2_ref[...]).astype(BF16), win_ref, proj_refs, splits, nchunk)
    else:
        (o_ref,) = refs[5:]
    o_ref[...] = x


def ple_and_proj(h, g, w_gate, p, w_proj, layer, nxt=None, tm=512, nchunk=512):
    m, d = h.shape
    in_specs = [_row_spec(tm, d), pl.BlockSpec((1, d), lambda i: (0, 0)), _layer_spec(w_gate, layer, True),
                pl.BlockSpec((None, tm, p.shape[2]), lambda i: (layer, i, 0)), _layer_spec(w_proj, layer, True)]
    args = [h, g.reshape(1, d), w_gate, p, w_proj]
    out_specs = [_row_spec(tm, d)]
    out_shape = [jax.ShapeDtypeStruct((m, d), F32)]
    splits = ()
    if nxt is not None:
        g2, w_in, in_layer, splits, dtypes = nxt
        assert sum(splits) == w_in.shape[2]
        in_specs += [pl.BlockSpec((1, d), lambda i: (0, 0)), _layer_spec(w_in, in_layer, True)]
        args += [g2.reshape(1, d), w_in]
        out_specs += [_row_spec(tm, s) for s in splits]
        out_shape += [jax.ShapeDtypeStruct((m, s), dt) for s, dt in zip(splits, dtypes)]
    return pl.pallas_call(
        functools.partial(_ple_proj_kernel, splits=tuple(splits), nchunk=nchunk),
        grid=(m // tm,),
        in_specs=in_specs,
        out_specs=out_specs,
        out_shape=out_shape,
        compiler_params=_cparams(("parallel",)),
        name="ple_proj",
    )(*args)


def _sb_tri():
    j = np.arange(2 * BLK)[:, None] % BLK
    s = np.arange(2 * BLK)[None, :]
    return ((s >= BLK) | (j >= s)).astype(np.float32)


def _neg_abs(x):
    sign = jnp.uint32(0x80000000)
    return lax.bitcast_convert_type(lax.bitcast_convert_type(x, jnp.uint32) | sign, F32)


def _sb_kernel(q_ref, k_ref, v_ref, tri_ref, o_ref, qs_ref, acc_ref, c_ref):
    n = pl.program_id(1)
    npair = SB_WIDTH // LANES
    pairs = range(npair)
    cols = [slice(p * LANES, (p + 1) * LANES) for p in pairs]
    lane = lax.broadcasted_iota(jnp.int32, (2 * BLK, BLK), 1)
    row = lax.broadcasted_iota(jnp.int32, (2 * BLK, BLK), 0)
    first_head = row < BLK
    causal = lane < jnp.where(first_head, row, row - BLK)
    own_lanes = first_head == (lane < SB_DIM)

    for p in pairs:
        q2 = q_ref[0, :, cols[p]] * (SB_DIM ** -0.5)
        q2 = jnp.concatenate([q2, q2], axis=0)
        qs_ref[p] = jnp.where(own_lanes, q2, jnp.zeros_like(q2))

    def stacked(ref, p, lo, hi):
        if (lo, hi) == (0, BLK):
            return ref[p]
        return jnp.concatenate([ref[p, lo:hi], ref[p, BLK + lo:BLK + hi]], axis=0)

    def log_keep_sums(t, mask):
        log_keep = jnp.minimum(t, 0.0) - jnp.log2(1.0 + jnp.exp2(_neg_abs(t)))
        if mask:
            log_keep = jnp.where(causal, log_keep, 0.0)
        return jnp.concatenate(_split_bf16(log_keep), axis=1)

    def process(kb, nblk, diagonal, lo=0, hi=BLK, near_kb=None):
        r, rn = hi - lo, SB_NEAR_ROWS
        near = near_kb is not None
        assert not near or (lo, hi) == (0, BLK)
        krows = pl.ds(pl.multiple_of(kb * BLK, BLK), nblk * BLK)
        first_half = lax.broadcasted_iota(jnp.int32, (r, BLK), 1) < SB_DIM
        tn = [_dot_nt(stacked(qs_ref, p, lo, hi), k_ref[0, krows, cols[p]]) * (-LOG2E) for p in pairs]
        if near:
            nrows = pl.ds(pl.multiple_of(near_kb * BLK, BLK), BLK)
            tn_n = [_dot_nt(stacked(qs_ref, p, 0, rn), k_ref[0, nrows, cols[p]]) * (-LOG2E) for p in pairs]
        order = list(range(nblk - 1, -1, -1))
        split = {(p, j): log_keep_sums(tn[p][:, j * BLK:(j + 1) * BLK], diagonal and j == nblk - 1)
                 for p in pairs for j in order}
        if near:
            split_n = [log_keep_sums(tn_n[p], False) for p in pairs]
        sums = {(p, j): _dot(split[p, j], tri_ref[...]) for p in pairs for j in order}
        if near:
            sums_n = [_dot(split_n[p], tri_ref[...]) for p in pairs]
        ws, ws_n = [], []
        for p in pairs:
            c = None if diagonal else stacked(c_ref, p, lo, hi)
            w = [None] * nblk
            for j in order:
                incl, total = sums[p, j][:, :BLK], sums[p, j][:, BLK:]
                arg = incl - tn[p][:, j * BLK:(j + 1) * BLK]
                if c is not None:
                    arg = arg + c
                wj = jnp.exp2(arg)
                if diagonal and j == nblk - 1:
                    wj = jnp.where(causal, wj, 0.0)
                w[j] = wj.astype(BF16)
                c = total if c is None else c + total
            ws.append(w[0] if nblk == 1 else jnp.concatenate(w, axis=1))
            if (lo, hi) == (0, BLK):
                c_ref[p] = c
            else:
                c_ref[p, lo:hi] = c[:r]
                c_ref[p, BLK + lo:BLK + hi] = c[r:]
            if near:
                c_n = jnp.concatenate([c[:rn], c[BLK:BLK + rn]], axis=0)
                ws_n.append(jnp.exp2(sums_n[p][:, :BLK] - tn_n[p] + c_n).astype(BF16))
                c_n = c_n + sums_n[p][:, BLK:]
                c_ref[p, 0:rn] = c_n[:rn]
                c_ref[p, BLK:BLK + rn] = c_n[rn:]
        pvs = [_dot(ws[p], v_ref[0, krows, cols[p]]) for p in pairs]
        if near:
            pvs_n = [_dot(ws_n[p], v_ref[0, nrows, cols[p]]) for p in pairs]
        for p in pairs:
            pv = jnp.where(first_half, pvs[p][:r], pvs[p][r:])
            if near:
                pv_n = jnp.where(lax.broadcasted_iota(jnp.int32, (rn, BLK), 1) < SB_DIM, pvs_n[p][:rn], pvs_n[p][rn:])
                pv = jnp.concatenate([pv[:rn] + pv_n, pv[rn:]], axis=0)
            if diagonal:
                acc_ref[p] = pv
            else:
                acc_ref[p, lo:hi] += pv

    def live(lo, hi):
        c = c_ref[...]
        return jnp.max(jnp.maximum(c[:, lo:hi], c[:, BLK + lo:BLK + hi])) > SB_DEAD_LOG2

    @pl.when(n == 0)
    def _():
        process(n, 1, True)

    @pl.when(n == 1)
    def _():
        process(n - 1, 2, True)

    def full_block(state):
        process(state[0], 1, False)
        return state[0] - 1, live(SB_NEAR_ROWS, BLK)

    def near_block(state):
        process(state[0], 1, False, 0, SB_NEAR_ROWS)
        return state[0] - 1, live(0, SB_NEAR_ROWS)

    more = lambda st: (st[0] >= 0) & st[1]

    @pl.when(n >= 2)
    def _():
        process(n - 1, 2, True, near_kb=n - 2)

        @pl.when(live(0, BLK))
        def _():
            @pl.when(live(SB_NEAR_ROWS, BLK))
            def _():
                process(n - 2, 1, False, SB_NEAR_ROWS, BLK)

            kb, _ = lax.while_loop(more, full_block, (n - 3, live(SB_NEAR_ROWS, BLK)))
            lax.while_loop(more, near_block, (kb, live(0, SB_NEAR_ROWS)))

    for p in pairs:
        o_ref[0, :, cols[p]] = acc_ref[p].astype(o_ref.dtype)


def stick_breaking(proj):
    b, s, _ = proj.shape
    tri = jnp.asarray(_sb_tri(), BF16)
    npair = SB_WIDTH // LANES
    return pl.pallas_call(
        _sb_kernel,
        grid=(b, s // BLK),
        in_specs=[pl.BlockSpec((1, BLK, SB_WIDTH), lambda i, n: (i, n, 0)),
                  pl.BlockSpec((1, s, SB_WIDTH), lambda i, n: (i, 0, 1)),
                  pl.BlockSpec((1, s, SB_WIDTH), lambda i, n: (i, 0, 2)),
                  pl.BlockSpec((2 * BLK, 2 * BLK), lambda i, n: (0, 0))],
        out_specs=pl.BlockSpec((1, BLK, SB_WIDTH), lambda i, n: (i, n, 0)),
        out_shape=jax.ShapeDtypeStruct((b, s, SB_WIDTH), BF16),
        scratch_shapes=[pltpu.VMEM((npair, 2 * BLK, LANES), BF16),
                        pltpu.VMEM((npair, BLK, LANES), F32),
                        pltpu.VMEM((npair, 2 * BLK, LANES), F32)],
        compiler_params=_cparams(("parallel", "arbitrary")),
        name="stick_breaking",
    )(proj, proj, proj, tri)


def _hg_time_mats():
    t = np.arange(BLK)[:, None]
    j = np.arange(BLK)[None, :]
    same = (t // HG_CHUNK) == (j // HG_CHUNK)
    per_chunk = np.arange(2 * SUBLANES)[:, None] == (j // HG_CHUNK)
    return np.concatenate([same & (j <= t), same, per_chunk], axis=0).astype(np.float32)


def _hgrn_kernel(q_ref, f_ref, i_ref, g_ref, lb_ref, gn_ref, tmat_ref, o_ref,
                 qf_s, kk_s, b_s, iv_s, qt_s, kt_s, dec_s, raw_s, state_s):
    ts = q_ref.shape[1]
    units = [(bi, h) for bi in range(q_ref.shape[0]) for h in range(HG_HEADS)]
    hcols = [slice(h * LANES, (h + 1) * LANES) for h in range(HG_HEADS)]
    half = HG_CHUNK // 2
    cpb = BLK // HG_CHUNK

    @pl.when(pl.program_id(1) == 0)
    def _():
        state_s[...] = jnp.zeros_like(state_s)

    def prep(r, _):
        rows = pl.ds(pl.multiple_of(r * BLK, BLK), BLK)
        for u, (bi, h) in enumerate(units):
            lb = lb_ref[:, hcols[h]]
            fp = f_ref[bi, rows, hcols[h]]
            q = q_ref[bi, rows, hcols[h]]
            e = jnp.exp(-jnp.abs(fp))
            sg = 1.0 / (1.0 + e)
            log_f = jnp.log(lb + (1.0 - lb) * jnp.where(fp >= 0, sg, e * sg))
            kk = (1.0 - lb) * jnp.where(fp >= 0, e * sg, sg)
            qf = q * jax.nn.sigmoid(q)
            hi, lo = _split_bf16(log_f)
            sums = _dot(tmat_ref[...], jnp.concatenate([hi, lo], axis=1))
            sums = sums[:, :LANES] + sums[:, LANES:]
            b = sums[:BLK]
            b_tot = sums[BLK:2 * BLK]
            chunk_tot = sums[2 * BLK:2 * BLK + cpb]
            qf_s[u, rows, :] = qf
            kk_s[u, rows, :] = kk
            b_s[u, rows, :] = b
            iv_s[u, rows, :] = i_ref[bi, rows, hcols[h]]
            qt_s[u, rows, :] = (qf * jnp.exp(b)).astype(BF16)
            kt_s[u, rows, :] = (kk * jnp.exp(b_tot - b)).astype(BF16)
            dec_s[u, pl.ds(pl.multiple_of(r * cpb, cpb), cpb), :] = jnp.exp(chunk_tot)
        return 0

    lax.fori_loop(0, ts // BLK, prep, 0)

    rowi = lax.broadcasted_iota(jnp.int32, (half, 1), 0)

    def chunk(c, _):
        r0 = pl.multiple_of(c * HG_CHUNK, HG_CHUNK)
        rows = pl.ds(r0, HG_CHUNK)
        lo_rows = pl.ds(r0, half)
        hi_rows = pl.ds(r0 + half, half)
        inter = [_dot_nt(qt_s[u, rows, :], state_s[u].astype(BF16)) for u in range(len(units))]
        for u in range(len(units)):
            qf_lo, qf_hi = qf_s[u, lo_rows, :], qf_s[u, hi_rows, :]
            b_lo, b_hi = b_s[u, lo_rows, :], b_s[u, hi_rows, :]
            o_lo = o_hi = jnp.zeros((half, LANES), F32)
            for s in range(HG_CHUNK):
                one = pl.ds(r0 + s, 1)
                b_key, k_key, v_key = b_s[u, one, :], kk_s[u, one, :], iv_s[u, one, :]
                if s < half:
                    score = jnp.sum(qf_lo * k_key * jnp.exp(b_lo - b_key), axis=-1, keepdims=True)
                    if s > 0:
                        score = jnp.where(rowi >= s, score, 0.0)
                    o_lo = o_lo + score * v_key
                score = jnp.sum(qf_hi * k_key * jnp.exp(b_hi - b_key), axis=-1, keepdims=True)
                if s > half:
                    score = jnp.where(rowi >= s - half, score, 0.0)
                o_hi = o_hi + score * v_key
            raw_s[u, lo_rows, :] = o_lo + inter[u][:half]
            raw_s[u, hi_rows, :] = o_hi + inter[u][half:]
        upd = [_dot_tn(iv_s[u, rows, :].astype(BF16), kt_s[u, rows, :]) for u in range(len(units))]
        for u in range(len(units)):
            state_s[u] = state_s[u] * dec_s[u, pl.ds(c, 1), :] + upd[u]
        return 0

    lax.fori_loop(0, ts // HG_CHUNK, chunk, 0)

    def finish(r, _):
        rows = pl.ds(pl.multiple_of(r * BLK, BLK), BLK)
        for u, (bi, h) in enumerate(units):
            g = g_ref[bi, rows, hcols[h]]
            y = _rms(raw_s[u, rows, :], gn_ref[...]) * (g * jax.nn.sigmoid(g))
            o_ref[bi, rows, hcols[h]] = y.astype(o_ref.dtype)
        return 0

    lax.fori_loop(0, ts // BLK, finish, 0)


def hgrn2(proj, lb, out_norm, ts=512):
    b, s, _ = proj.shape
    assert s % ts == 0 and ts % BLK == 0 and b % HG_BATCH == 0
    tmat = jnp.asarray(_hg_time_mats(), BF16)
    seq_blk = lambda part: pl.BlockSpec((HG_BATCH, ts, HG_QK), lambda i, t: (i, t, part))
    const = lambda i, t: (0, 0)
    nu = HG_BATCH * HG_HEADS
    return pl.pallas_call(
        _hgrn_kernel,
        grid=(b // HG_BATCH, s // ts),
        in_specs=[seq_blk(0), seq_blk(1), seq_blk(2), seq_blk(3),
                  pl.BlockSpec((1, HG_QK), const),
                  pl.BlockSpec((1, HG_DV), const),
                  pl.BlockSpec(tmat.shape, const)],
        out_specs=pl.BlockSpec((HG_BATCH, ts, HG_V), lambda i, t: (i, t, 0)),
        out_shape=jax.ShapeDtypeStruct((b, s, HG_V), BF16),
        scratch_shapes=[pltpu.VMEM((nu, ts, HG_DK), F32), pltpu.VMEM((nu, ts, HG_DK), F32),
                        pltpu.VMEM((nu, ts, HG_DK), F32), pltpu.VMEM((nu, ts, HG_DV), F32),
                        pltpu.VMEM((nu, ts, HG_DK), BF16), pltpu.VMEM((nu, ts, HG_DK), BF16),
                        pltpu.VMEM((nu, ts // HG_CHUNK, HG_DK), F32),
                        pltpu.VMEM((nu, ts, HG_DV), F32),
                        pltpu.VMEM((nu, HG_DV, HG_DK), F32)],
        compiler_params=_cparams(("parallel", "arbitrary")),
        name="hgrn2",
    )(proj, proj, proj, proj, lb.reshape(1, HG_QK), out_norm.reshape(1, HG_DV), tmat)


def _t5_band_buckets():
    t = np.arange(WINDOW)[:, None]
    s = np.arange(2 * WINDOW)[None, :]
    dist = t + WINDOW - s
    max_exact = N_BUCKETS // 2
    large = max_exact + (np.log(np.maximum(dist, max_exact) / max_exact) / math.log(MAX_DISTANCE / max_exact)
                         * (N_BUCKETS - max_exact)).astype(np.int32)
    large = np.minimum(large, N_BUCKETS - 1)
    band = (dist >= 0) & (dist < WINDOW)
    return np.where(dist < max_exact, np.maximum(dist, 0), large).astype(np.int32), band


def _swa_kernel(sink_ref, q_ref, kc_ref, kp_ref, vc_ref, vp_ref, bias_ref, qg_ref, kg_ref, gmat_ref, o_ref):
    n = pl.program_id(1)
    w2 = 2 * WINDOW
    no_prev = jnp.where(n > 0, 0.0, -jnp.inf)
    q_groups = SW_HEADS * SW_DIM // LANES
    kv_groups = SW_KV_HEADS * SW_DIM // LANES
    per_kv = q_groups // kv_groups
    kv_rows = (SW_QBLK + 1) * WINDOW
    grp = lambda ref, p: ref[0, :, p * LANES:(p + 1) * LANES]

    def head_norm(x, gain):
        sq = jnp.concatenate(_split_bf16(x * x), axis=1)
        ssq = _dot(sq, gmat_ref[...])
        return x * lax.rsqrt(ssq * (1.0 / SW_DIM) + EPS) * gain

    qn = head_norm(jnp.concatenate([grp(q_ref, p)[i * WINDOW:(i + 1) * WINDOW]
                                    for i in range(SW_QBLK) for p in range(q_groups)], axis=0), qg_ref[...])
    kn = head_norm(jnp.concatenate([x for r in range(kv_groups) for x in (grp(kp_ref, r), grp(kc_ref, r))], axis=0),
                   kg_ref[...])
    v = jnp.concatenate([x for r in range(kv_groups) for x in (grp(vp_ref, r), grp(vc_ref, r))], axis=0)
    kn_rot, v_rot = pltpu.roll(kn, SW_DIM, 1), pltpu.roll(v, SW_DIM, 1)
    kn, kn_rot, v, v_rot = (a.astype(BF16) for a in (kn, kn_rot, v, v_rot))
    ones = jnp.ones((w2, LANES), BF16)

    rows_q = per_kv * WINDOW
    lane = lax.broadcasted_iota(jnp.int32, (rows_q, LANES), 1)
    row = lax.broadcasted_iota(jnp.int32, (rows_q, LANES), 0)
    same_half = (lane < SW_DIM) == (row < rows_q // 2)
    blocks = [(i, r) for i in range(SW_QBLK) for r in range(kv_groups)]
    keys = lambda i, r: slice(r * kv_rows + i * WINDOW, r * kv_rows + i * WINDOW + w2)

    logits = {}
    for i, r in blocks:
        qr = qn[(i * kv_groups + r) * rows_q:(i * kv_groups + r + 1) * rows_q]
        zero = jnp.zeros_like(qr)
        logits[i, r] = (_dot_nt(jnp.where(same_half, qr, zero).astype(BF16), kn[keys(i, r)]),
                        _dot_nt(jnp.where(same_half, zero, qr).astype(BF16), kn_rot[keys(i, r)]))

    weights, sink_terms = {}, {}
    for i, r in blocks:
        w_r, s_r = ([], []), ([], [])
        for j in range(per_kv):
            p = r * per_kv + j
            second = j >= per_kv // 2
            for variant, h in ((0, 2 * p + int(second)), (1, 2 * p + int(not second))):
                lg = logits[i, r][variant][j * WINDOW:(j + 1) * WINDOW]
                bias_prev = bias_ref[h, :, :WINDOW]
                lp = lg[:, :WINDOW] + (bias_prev + no_prev if i == 0 else bias_prev)
                lc = lg[:, WINDOW:] + bias_ref[h, :, WINDOW:]
                sink = sink_ref[h]
                m = jnp.maximum(jnp.max(jnp.maximum(lp, lc), axis=-1, keepdims=True), sink)
                w_r[variant].append(jnp.concatenate([jnp.exp(lp - m), jnp.exp(lc - m)], axis=1).astype(BF16))
                s_r[variant].append(jnp.broadcast_to(jnp.exp(sink - m), (WINDOW, LANES)))
        weights[i, r] = tuple(jnp.concatenate(x, axis=0) for x in w_r)
        sink_terms[i, r] = tuple(jnp.concatenate(x, axis=0) for x in s_r)

    outs = {(i, r): (_dot(weights[i, r][0], jnp.concatenate([v[keys(i, r)], ones], axis=1)),
                     _dot(weights[i, r][1], jnp.concatenate([v_rot[keys(i, r)], ones], axis=1)))
            for i, r in blocks}
    for i, r in blocks:
        o = [outs[i, r][t][:, :LANES] / (outs[i, r][t][:, LANES:] + sink_terms[i, r][t]) for t in range(2)]
        o = jnp.where(same_half, o[0], o[1]).astype(o_ref.dtype)
        for j in range(per_kv):
            p = r * per_kv + j
            o_ref[0, i * WINDOW:(i + 1) * WINDOW, p * LANES:(p + 1) * LANES] = o[j * WINDOW:(j + 1) * WINDOW]


def _swa_group_sum():
    i = np.arange(2 * LANES)[:, None] % LANES
    j = np.arange(LANES)[None, :]
    return (i // SW_DIM == j // SW_DIM).astype(np.float32)


def sliding_window(proj, q_norm, k_norm, sinks, rel_bias):
    b, s, _ = proj.shape
    bucket, band = _t5_band_buckets()
    onehot = jnp.asarray(bucket[..., None] == np.arange(N_BUCKETS), F32)
    bias = jnp.einsum('tsb,bh->hts', onehot, rel_bias.astype(F32), precision=lax.Precision.HIGHEST)
    bias = jnp.where(band, bias, -jnp.inf)
    kv_w = SW_KV_HEADS * SW_DIM
    q_w = SW_HEADS * SW_DIM
    k_blk = q_w // kv_w
    per_lane_group = LANES // SW_DIM
    q_gain = jnp.tile(q_norm.astype(F32), per_lane_group).reshape(1, LANES) * (SW_DIM ** -0.5)
    k_gain = jnp.tile(k_norm.astype(F32), per_lane_group).reshape(1, LANES)
    tq = SW_QBLK * WINDOW
    assert s % tq == 0
    prev = lambda n: jnp.maximum(SW_QBLK * n - 1, 0)
    const2 = lambda i, n: (0, 0)
    return pl.pallas_call(
        _swa_kernel,
        grid=(b, s // tq),
        in_specs=[pl.BlockSpec(memory_space=pltpu.SMEM),
                  pl.BlockSpec((1, tq, q_w), lambda i, n: (i, n, 0)),
                  pl.BlockSpec((1, tq, kv_w), lambda i, n: (i, n, k_blk)),
                  pl.BlockSpec((1, WINDOW, kv_w), lambda i, n: (i, prev(n), k_blk)),
                  pl.BlockSpec((1, tq, kv_w), lambda i, n: (i, n, k_blk + 1)),
                  pl.BlockSpec((1, WINDOW, kv_w), lambda i, n: (i, prev(n), k_blk + 1)),
                  pl.BlockSpec((SW_HEADS, WINDOW, 2 * WINDOW), lambda i, n: (0, 0, 0)),
                  pl.BlockSpec((1, LANES), const2),
                  pl.BlockSpec((1, LANES), const2),
                  pl.BlockSpec((2 * LANES, LANES), const2)],
        out_specs=pl.BlockSpec((1, tq, q_w), lambda i, n: (i, n, 0)),
        out_shape=jax.ShapeDtypeStruct((b, s, q_w), BF16),
        compiler_params=_cparams(("parallel", "parallel")),
        name="sliding_window",
    )(sinks.astype(F32), proj, proj, proj, proj, proj, bias, q_gain, k_gain,
      jnp.asarray(_swa_group_sum(), BF16))


def kernel(x, p, mix_norm, ab_w_in, hg_lb_logits, hg_out_norm, ab_w_out, c_w_in, q_norm, k_norm, sinks,
           rel_bias, c_w_out, ffn_norm, ffn_up, ffn_conv, ffn_conv_b, ffn_down, ple_norm, ple_gate, ple_proj):
    b, s, d = x.shape
    depth = p.shape[0]
    m = b * s
    lb_cum = jnp.cumsum(jax.nn.softmax(hg_lb_logits.astype(F32), axis=0), axis=0)
    lower_bounds = lb_cum - lb_cum[0]
    ab_w_in, ab_w_out, c_w_in, c_w_out, ffn_up, ffn_down, ple_gate, ple_proj = (
        w.astype(BF16) for w in (ab_w_in, ab_w_out, c_w_in, c_w_out, ffn_up, ffn_down, ple_gate, ple_proj))
    conv_b = ffn_conv_b.reshape(depth, 1, -1)
    p = p.reshape(depth, m, -1)

    def in_proj(i):
        if i % 2 == 0:
            return mix_norm[i], ab_w_in, i // 2, (3 * SB_WIDTH, 2 * HG_QK + 2 * HG_V), (BF16, F32)
        return mix_norm[i], c_w_in, i // 2, (c_w_in.shape[2],), (F32,)

    h = x.reshape(m, d)
    proj = norm_proj(h, *in_proj(0))
    for i in range(depth):
        j = i // 2
        if i % 2 == 0:
            att, gates = proj
            o_a = stick_breaking(att.reshape(b, s, -1))
            o_b = hgrn2(gates.reshape(b, s, -1), lower_bounds[j], hg_out_norm[j])
            parts, w_out = [o_a.reshape(m, -1), o_b.reshape(m, -1)], ab_w_out
        else:
            (qkv,) = proj
            o = sliding_window(qkv.reshape(b, s, -1), q_norm[j], k_norm[j], sinks[j], rel_bias)
            parts, w_out = [o.reshape(m, -1)], c_w_out
        h = mix_ffn(parts, w_out, j, h, ffn_norm[i], ffn_up, ffn_conv, conv_b, ffn_down, i, s)
        h, *proj = ple_and_proj(h, ple_norm[i], ple_gate, p, ple_proj, i, in_proj(i + 1) if i + 1 < depth else None)
    return h.reshape(b, s, d)
```

```python
import functools
import math

import numpy as np
import jax
import jax.numpy as jnp
from jax import lax
from jax.experimental import pallas as pl
from jax.experimental.pallas import tpu as pltpu

F32 = jnp.float32
BF16 = jnp.bfloat16

D_MODEL = 1024
PLE_DIM = 256
EPS = 1e-6
SB_HEADS = 8
SB_DIM = 64
SB_WIDTH = SB_HEADS * SB_DIM
HG_HEADS = 4
HG_DK = 128
HG_DV = 128
HG_QK = HG_HEADS * HG_DK
HG_V = HG_HEADS * HG_DV
SW_HEADS = 16
SW_KV_HEADS = 4
SW_DIM = 64
SW_GROUP = SW_HEADS // SW_KV_HEADS
WINDOW = 128
SW_QBLK = 2
N_BUCKETS = 32
MAX_DISTANCE = 128
D_FF = 2816
CONV_W = 3

LANES = 128
SUBLANES = 8
BLK = 128
HG_CHUNK = 16
HG_BATCH = 2
FF_CHUNK = 256
SB_DEAD_LOG2 = -150.0
SB_NEAR_ROWS = 32
SB_QBLK = 2
LOG2E = 1.4426950408889634
VMEM_LIMIT = 56 * 1024 * 1024


def _cparams(sem):
    return pltpu.CompilerParams(dimension_semantics=sem, vmem_limit_bytes=VMEM_LIMIT)


def _rms(x, g):
    ms = jnp.mean(x * x, axis=-1, keepdims=True)
    return x * lax.rsqrt(ms + EPS) * g


def _dot(a, b):
    return jnp.dot(a, b, preferred_element_type=F32)


def _dot_nt(a, b):
    return lax.dot_general(a, b, (((1,), (1,)), ((), ())), preferred_element_type=F32)


def _dot_tn(a, b):
    return lax.dot_general(a, b, (((0,), (0,)), ((), ())), preferred_element_type=F32)


def _split_bf16(x):
    hi = x.astype(BF16)
    lo = (x - hi.astype(F32)).astype(BF16)
    return hi, lo


def _layer_spec(stack, layer, single_buffer=False):
    zeros = (0,) * (stack.ndim - 1)
    kwargs = dict(pipeline_mode=pl.Buffered(1)) if single_buffer else {}
    return pl.BlockSpec((None,) + stack.shape[1:], lambda *_: (layer,) + zeros, **kwargs)


def _row_spec(tm, width):
    return pl.BlockSpec((tm, width), lambda i: (i, 0))


def _project(hn, w_ref, o_refs, splits, nchunk):
    col = 0
    for o_ref, width in zip(o_refs, splits):
        for c0 in range(0, width, nchunk):
            cw = min(nchunk, width - c0)
            o_ref[:, c0:c0 + cw] = _dot(hn, w_ref[:, col + c0:col + c0 + cw]).astype(o_ref.dtype)
        col += width


def _norm_proj_kernel(h_ref, g_ref, w_ref, *o_refs, splits, nchunk):
    _project(_rms(h_ref[...], g_ref[...]).astype(BF16), w_ref, o_refs, splits, nchunk)


def norm_proj(h, g, w_stack, layer, splits, dtypes, tm=512, nchunk=512):
    m, k = h.shape
    assert sum(splits) == w_stack.shape[2] and m % tm == 0
    return pl.pallas_call(
        functools.partial(_norm_proj_kernel, splits=tuple(splits), nchunk=nchunk),
        grid=(m // tm,),
        in_specs=[_row_spec(tm, k), pl.BlockSpec((1, k), lambda i: (0, 0)), _layer_spec(w_stack, layer)],
        out_specs=[_row_spec(tm, s) for s in splits],
        out_shape=[jax.ShapeDtypeStruct((m, s), d) for s, d in zip(splits, dtypes)],
        compiler_params=_cparams(("parallel",)),
        name="norm_proj",
    )(h, g.reshape(1, k), w_stack)


def _mix_ffn_kernel(*refs, n_parts, tiles_per_seq):
    a_refs = refs[:n_parts]
    wo_ref, h_ref, g_ref, wup_ref, cw_ref, cb_ref, wdn_ref, o_ref, hn_ref, act_ref, halo_ref = refs[n_parts:]
    tm = h_ref.shape[0]
    x = h_ref[...]
    k0 = 0
    for a_ref in a_refs:
        k = a_ref.shape[1]
        x = x + _dot(a_ref[...], wo_ref[k0:k0 + k, :])
        k0 += k
    hn_ref[...] = _rms(x, g_ref[...]).astype(BF16)
    seq_start = (pl.program_id(0) % tiles_per_seq) == 0
    row = lax.broadcasted_iota(jnp.int32, (tm, FF_CHUNK), 0)

    def conv_cols(c0):
        cols = slice(c0, c0 + FF_CHUNK)
        u = _dot(hn_ref[...], wup_ref[:, cols])
        prev = jnp.where(seq_start, 0.0, halo_ref[:, cols])
        halo_ref[:, cols] = u[tm - SUBLANES:, :]
        p1 = prev[SUBLANES - 1:SUBLANES, :]
        p2 = prev[SUBLANES - 2:SUBLANES - 1, :]
        u1 = jnp.where(row == 0, p1, pltpu.roll(u, 1, 0))
        u2 = jnp.where(row == 0, p2, jnp.where(row == 1, p1, pltpu.roll(u, 2, 0)))
        cw = cw_ref[:, cols]
        return cw[0:1, :] * u2 + cw[1:2, :] * u1 + cw[2:3, :] * u + cb_ref[:, cols]

    for c in range(D_FF // FF_CHUNK):
        gate = conv_cols(c * FF_CHUNK)
        up = conv_cols(D_FF + c * FF_CHUNK)
        act_ref[:, c * FF_CHUNK:(c + 1) * FF_CHUNK] = (gate * jax.nn.sigmoid(gate) * up).astype(BF16)
    o_ref[...] = x + _dot(act_ref[...], wdn_ref[...])


def mix_ffn(parts, w_out, out_layer, h, g, w_up, conv_w, conv_b, w_down, layer, seq, tm=1024):
    m, d = h.shape
    f2 = w_up.shape[2]
    assert seq % tm == 0 and D_FF % FF_CHUNK == 0 and sum(a.shape[1] for a in parts) == w_out.shape[1]
    return pl.pallas_call(
        functools.partial(_mix_ffn_kernel, n_parts=len(parts), tiles_per_seq=seq // tm),
        grid=(m // tm,),
        in_specs=[_row_spec(tm, a.shape[1]) for a in parts]
                 + [_layer_spec(w_out, out_layer, True), _row_spec(tm, d), pl.BlockSpec((1, d), lambda i: (0, 0)),
                    _layer_spec(w_up, layer, True), _layer_spec(conv_w, layer), _layer_spec(conv_b, layer),
                    _layer_spec(w_down, layer, True)],
        out_specs=_row_spec(tm, d),
        out_shape=jax.ShapeDtypeStruct((m, d), F32),
        scratch_shapes=[pltpu.VMEM((tm, d), BF16),
                        pltpu.VMEM((tm, D_FF), BF16),
                        pltpu.VMEM((SUBLANES, f2), F32)],
        compiler_params=_cparams(("arbitrary",)),
        name="mix_ffn",
    )(*parts, w_out, h, g.reshape(1, d), w_up, conv_w, conv_b, w_down)


def _ple_proj_kernel(*refs, splits, nchunk):
    h_ref, g_ref, wg_ref, p_ref, wp_ref = refs[:5]
    x = h_ref[...]
    gate = jax.nn.sigmoid(_dot(_rms(x, g_ref[...]).astype(BF16), wg_ref[...]))
    x = x + gate * _dot(p_ref[...].astype(BF16), wp_ref[...])
    if splits:
        g2_ref, win_ref, o_ref, *proj_refs = refs[5:]
        _project(_rms(x, g2_ref[...]).astype(BF16), win_ref, proj_refs, splits, nchunk)
    else:
        (o_ref,) = refs[5:]
    o_ref[...] = x


def ple_and_proj(h, g, w_gate, p, w_proj, layer, nxt=None, tm=512, nchunk=512):
    m, d = h.shape
    in_specs = [_row_spec(tm, d), pl.BlockSpec((1, d), lambda i: (0, 0)), _layer_spec(w_gate, layer, True),
                pl.BlockSpec((None, tm, p.shape[2]), lambda i: (layer, i, 0)), _layer_spec(w_proj, layer, True)]
    args = [h, g.reshape(1, d), w_gate, p, w_proj]
    out_specs = [_row_spec(tm, d)]
    out_shape = [jax.ShapeDtypeStruct((m, d), F32)]
    splits = ()
    if nxt is not None:
        g2, w_in, in_layer, splits, dtypes = nxt
        assert sum(splits) == w_in.shape[2]
        in_specs += [pl.BlockSpec((1, d), lambda i: (0, 0)), _layer_spec(w_in, in_layer, True)]
        args += [g2.reshape(1, d), w_in]
        out_specs += [_row_spec(tm, s) for s in splits]
        out_shape += [jax.ShapeDtypeStruct((m, s), dt) for s, dt in zip(splits, dtypes)]
    return pl.pallas_call(
        functools.partial(_ple_proj_kernel, splits=tuple(splits), nchunk=nchunk),
        grid=(m // tm,),
        in_specs=in_specs,
        out_specs=out_specs,
        out_shape=out_shape,
        compiler_params=_cparams(("parallel",)),
        name="ple_proj",
    )(*args)


def _sb_tri():
    j = np.arange(2 * BLK)[:, None] % BLK
    s = np.arange(2 * BLK)[None, :]
    return ((s >= BLK) | (j >= s)).astype(np.float32)


def _neg_abs(x):
    sign = jnp.uint32(0x80000000)
    return lax.bitcast_convert_type(lax.bitcast_convert_type(x, jnp.uint32) | sign, F32)


def _sb_kernel(q_ref, k_ref, v_ref, tri_ref, o_ref, qs_ref, acc_ref, c_ref):
    n0 = SB_QBLK * pl.program_id(1)
    npair = SB_WIDTH // LANES
    pairs = range(npair)
    cols = [slice(p * LANES, (p + 1) * LANES) for p in pairs]
    lane = lax.broadcasted_iota(jnp.int32, (2 * BLK, BLK), 1)
    row = lax.broadcasted_iota(jnp.int32, (2 * BLK, BLK), 0)
    first_head = row < BLK
    causal = lane < jnp.where(first_head, row, row - BLK)
    own_lanes = first_head == (lane < SB_DIM)
    rn = SB_NEAR_ROWS

    for qb in range(SB_QBLK):
        for p in pairs:
            q2 = q_ref[0, qb * BLK:(qb + 1) * BLK, cols[p]] * (SB_DIM ** -0.5)
            q2 = jnp.concatenate([q2, q2], axis=0)
            qs_ref[qb, p] = jnp.where(own_lanes, q2, jnp.zeros_like(q2))

    def stacked(ref, qb, p, lo, hi):
        if (lo, hi) == (0, BLK):
            return ref[qb, p]
        return jnp.concatenate([ref[qb, p, lo:hi], ref[qb, p, BLK + lo:BLK + hi]], axis=0)

    def run(jobs):
        for job in jobs:
            job["krows"] = pl.ds(pl.multiple_of(job["kb"] * BLK, BLK), job["nblk"] * BLK)
            job["tn"] = [_dot_nt(stacked(qs_ref, job["qb"], p, job["lo"], job["hi"]),
                                 k_ref[0, job["krows"], cols[p]]) * (-LOG2E) for p in pairs]
        for job in jobs:
            job["split"] = {}
            for p in pairs:
                for j in range(job["nblk"]):
                    t = job["tn"][p][:, j * BLK:(j + 1) * BLK]
                    log_keep = jnp.minimum(t, 0.0) - jnp.log2(1.0 + jnp.exp2(_neg_abs(t)))
                    if job["diag"] and j == job["nblk"] - 1:
                        log_keep = jnp.where(causal, log_keep, 0.0)
                    job["split"][p, j] = jnp.concatenate(_split_bf16(log_keep), axis=1)
        for job in jobs:
            job["sums"] = {key: _dot(val, tri_ref[...]) for key, val in job["split"].items()}
        for job in jobs:
            qb, lo, hi, nblk = job["qb"], job["lo"], job["hi"], job["nblk"]
            r = hi - lo
            job["ws"], job["c"] = [], []
            for p in pairs:
                if job["diag"]:
                    c = None
                elif job["after"] is not None:
                    prev = jobs[job["after"]]["c"][p]
                    c = jnp.concatenate([prev[lo:hi], prev[BLK + lo:BLK + hi]], axis=0)
                else:
                    c = stacked(c_ref, qb, p, lo, hi)
                w = [None] * nblk
                for j in range(nblk - 1, -1, -1):
                    s = job["sums"][p, j]
                    incl, total = s[:, :BLK], s[:, BLK:]
                    arg = incl - job["tn"][p][:, j * BLK:(j + 1) * BLK]
                    if c is not None:
                        arg = arg + c
                    wj = jnp.exp2(arg)
                    if job["diag"] and j == nblk - 1:
                        wj = jnp.where(causal, wj, 0.0)
                    w[j] = wj.astype(BF16)
                    c = total if c is None else c + total
                job["ws"].append(w[0] if nblk == 1 else jnp.concatenate(w, axis=1))
                job["c"].append(c)
                if (lo, hi) == (0, BLK):
                    c_ref[qb, p] = c
                else:
                    c_ref[qb, p, lo:hi] = c[:r]
                    c_ref[qb, p, BLK + lo:BLK + hi] = c[r:]
        for job in jobs:
            job["pv"] = [_dot(job["ws"][p], v_ref[0, job["krows"], cols[p]]) for p in pairs]
        for job in jobs:
            qb, lo, hi = job["qb"], job["lo"], job["hi"]
            r = hi - lo
            first_half = lax.broadcasted_iota(jnp.int32, (r, BLK), 1) < SB_DIM
            for p in pairs:
                pv = jnp.where(first_half, job["pv"][p][:r], job["pv"][p][r:])
                if job["diag"]:
                    acc_ref[qb, p] = pv
                else:
                    acc_ref[qb, p, lo:hi] += pv

    def job(qb, kb, nblk, diag, lo=0, hi=BLK, after=None):
        return dict(qb=qb, kb=kb, nblk=nblk, diag=diag, lo=lo, hi=hi, after=after)

    def live(qb, lo, hi):
        c = c_ref[qb]
        return jnp.max(jnp.maximum(c[:, lo:hi], c[:, BLK + lo:BLK + hi])) > SB_DEAD_LOG2

    more = lambda st: (st[0] >= 0) & st[1]

    def rest(qb, kb):
        @pl.when(live(qb, rn, BLK))
        def _():
            run([job(qb, kb, 1, False, rn, BLK)])

        def full_block(state):
            run([job(qb, state[0], 1, False)])
            return state[0] - 1, live(qb, rn, BLK)

        def near_block(state):
            run([job(qb, state[0], 1, False, 0, rn)])
            return state[0] - 1, live(qb, 0, rn)

        kb2, _ = lax.while_loop(more, full_block, (kb - 1, live(qb, rn, BLK)))
        lax.while_loop(more, near_block, (kb2, live(qb, 0, rn)))

    @pl.when(n0 == 0)
    def _():
        run([job(qb, 0, qb + 1, True) for qb in range(SB_QBLK)])

    @pl.when(n0 > 0)
    def _():
        jobs = []
        for qb in range(SB_QBLK):
            jobs.append(job(qb, n0 + qb - 1, 2, True))
            jobs.append(job(qb, n0 + qb - 2, 1, False, 0, rn, after=len(jobs) - 1))
        run(jobs)

        @pl.when(jnp.max(c_ref[...]) > SB_DEAD_LOG2)
        def _():
            for qb in range(SB_QBLK):
                rest(qb, n0 + qb - 2)

    for qb in range(SB_QBLK):
        for p in pairs:
            o_ref[0, qb * BLK:(qb + 1) * BLK, cols[p]] = acc_ref[qb, p].astype(o_ref.dtype)


def stick_breaking(proj):
    b, s, _ = proj.shape
    tri = jnp.asarray(_sb_tri(), BF16)
    npair = SB_WIDTH // LANES
    tq = SB_QBLK * BLK
    assert s % tq == 0 and SB_QBLK == 2
    return pl.pallas_call(
        _sb_kernel,
        grid=(b, s // tq),
        in_specs=[pl.BlockSpec((1, tq, SB_WIDTH), lambda i, n: (i, n, 0)),
                  pl.BlockSpec((1, s, SB_WIDTH), lambda i, n: (i, 0, 1)),
                  pl.BlockSpec((1, s, SB_WIDTH), lambda i, n: (i, 0, 2)),
                  pl.BlockSpec((2 * BLK, 2 * BLK), lambda i, n: (0, 0))],
        out_specs=pl.BlockSpec((1, tq, SB_WIDTH), lambda i, n: (i, n, 0)),
        out_shape=jax.ShapeDtypeStruct((b, s, SB_WIDTH), BF16),
        scratch_shapes=[pltpu.VMEM((SB_QBLK, npair, 2 * BLK, LANES), BF16),
                        pltpu.VMEM((SB_QBLK, npair, BLK, LANES), F32),
                        pltpu.VMEM((SB_QBLK, npair, 2 * BLK, LANES), F32)],
        compiler_params=_cparams(("parallel", "arbitrary")),
        name="stick_breaking",
    )(proj, proj, proj, tri)


def _hg_time_mats():
    t = np.arange(BLK)[:, None]
    j = np.arange(BLK)[None, :]
    same = (t // HG_CHUNK) == (j // HG_CHUNK)
    per_chunk = np.arange(2 * SUBLANES)[:, None] == (j // HG_CHUNK)
    return np.concatenate([same & (j <= t), same, per_chunk], axis=0).astype(np.float32)


def _hgrn_kernel(q_ref, f_ref, i_ref, g_ref, lb_ref, gn_ref, tmat_ref, o_ref,
                 qf_s, kk_s, b_s, iv_s, qt_s, kt_s, dec_s, raw_s, state_s):
    ts = q_ref.shape[1]
    units = [(bi, h) for bi in range(q_ref.shape[0]) for h in range(HG_HEADS)]
    hcols = [slice(h * LANES, (h + 1) * LANES) for h in range(HG_HEADS)]
    half = HG_CHUNK // 2
    cpb = BLK // HG_CHUNK

    @pl.when(pl.program_id(1) == 0)
    def _():
        state_s[...] = jnp.zeros_like(state_s)

    def prep(r, _):
        rows = pl.ds(pl.multiple_of(r * BLK, BLK), BLK)
        for u, (bi, h) in enumerate(units):
            lb = lb_ref[:, hcols[h]]
            fp = f_ref[bi, rows, hcols[h]]
            q = q_ref[bi, rows, hcols[h]]
            e = jnp.exp(-jnp.abs(fp))
            sg = 1.0 / (1.0 + e)
            log_f = jnp.log(lb + (1.0 - lb) * jnp.where(fp >= 0, sg, e * sg))
            kk = (1.0 - lb) * jnp.where(fp >= 0, e * sg, sg)
            qf = q * jax.nn.sigmoid(q)
            hi, lo = _split_bf16(log_f)
            sums = _dot(tmat_ref[...], jnp.concatenate([hi, lo], axis=1))
            sums = sums[:, :LANES] + sums[:, LANES:]
            b = sums[:BLK]
            b_tot = sums[BLK:2 * BLK]
            chunk_tot = sums[2 * BLK:2 * BLK + cpb]
            qf_s[u, rows, :] = qf
            kk_s[u, rows, :] = kk
            b_s[u, rows, :] = b
            iv_s[u, rows, :] = i_ref[bi, rows, hcols[h]]
            qt_s[u, rows, :] = (qf * jnp.exp(b)).astype(BF16)
            kt_s[u, rows, :] = (kk * jnp.exp(b_tot - b)).astype(BF16)
            dec_s[u, pl.ds(pl.multiple_of(r * cpb, cpb), cpb), :] = jnp.exp(chunk_tot)
        return 0

    lax.fori_loop(0, ts // BLK, prep, 0)

    rowi = lax.broadcasted_iota(jnp.int32, (half, 1), 0)

    def chunk(c, _):
        r0 = pl.multiple_of(c * HG_CHUNK, HG_CHUNK)
        rows = pl.ds(r0, HG_CHUNK)
        lo_rows = pl.ds(r0, half)
        hi_rows = pl.ds(r0 + half, half)
        inter = [_dot_nt(qt_s[u, rows, :], state_s[u].astype(BF16)) for u in range(len(units))]
        for u in range(len(units)):
            qf_lo, qf_hi = qf_s[u, lo_rows, :], qf_s[u, hi_rows, :]
            b_lo, b_hi = b_s[u, lo_rows, :], b_s[u, hi_rows, :]
            o_lo = o_hi = jnp.zeros((half, LANES), F32)
            for s in range(HG_CHUNK):
                one = pl.ds(r0 + s, 1)
                b_key, k_key, v_key = b_s[u, one, :], kk_s[u, one, :], iv_s[u, one, :]
                if s < half:
                    score = jnp.sum(qf_lo * k_key * jnp.exp(b_lo - b_key), axis=-1, keepdims=True)
                    if s > 0:
                        score = jnp.where(rowi >= s, score, 0.0)
                    o_lo = o_lo + score * v_key
                score = jnp.sum(qf_hi * k_key * jnp.exp(b_hi - b_key), axis=-1, keepdims=True)
                if s > half:
                    score = jnp.where(rowi >= s - half, score, 0.0)
                o_hi = o_hi + score * v_key
            raw_s[u, lo_rows, :] = o_lo + inter[u][:half]
            raw_s[u, hi_rows, :] = o_hi + inter[u][half:]
        upd = [_dot_tn(iv_s[u, rows, :].astype(BF16), kt_s[u, rows, :]) for u in range(len(units))]
        for u in range(len(units)):
            state_s[u] = state_s[u] * dec_s[u, pl.ds(c, 1), :] + upd[u]
        return 0

    lax.fori_loop(0, ts // HG_CHUNK, chunk, 0)

    def finish(r, _):
        rows = pl.ds(pl.multiple_of(r * BLK, BLK), BLK)
        for u, (bi, h) in enumerate(units):
            g = g_ref[bi, rows, hcols[h]]
            y = _rms(raw_s[u, rows, :], gn_ref[...]) * (g * jax.nn.sigmoid(g))
            o_ref[bi, rows, hcols[h]] = y.astype(o_ref.dtype)
        return 0

    lax.fori_loop(0, ts // BLK, finish, 0)


def hgrn2(proj, lb, out_norm, ts=512):
    b, s, _ = proj.shape
    assert s % ts == 0 and ts % BLK == 0 and b % HG_BATCH == 0
    tmat = jnp.asarray(_hg_time_mats(), BF16)
    seq_blk = lambda part: pl.BlockSpec((HG_BATCH, ts, HG_QK), lambda i, t: (i, t, part))
    const = lambda i, t: (0, 0)
    nu = HG_BATCH * HG_HEADS
    return pl.pallas_call(
        _hgrn_kernel,
        grid=(b // HG_BATCH, s // ts),
        in_specs=[seq_blk(0), seq_blk(1), seq_blk(2), seq_blk(3),
                  pl.BlockSpec((1, HG_QK), const),
                  pl.BlockSpec((1, HG_DV), const),
                  pl.BlockSpec(tmat.shape, const)],
        out_specs=pl.BlockSpec((HG_BATCH, ts, HG_V), lambda i, t: (i, t, 0)),
        out_shape=jax.ShapeDtypeStruct((b, s, HG_V), BF16),
        scratch_shapes=[pltpu.VMEM((nu, ts, HG_DK), F32), pltpu.VMEM((nu, ts, HG_DK), F32),
                        pltpu.VMEM((nu, ts, HG_DK), F32), pltpu.VMEM((nu, ts, HG_DV), F32),
                        pltpu.VMEM((nu, ts, HG_DK), BF16), pltpu.VMEM((nu, ts, HG_DK), BF16),
                        pltpu.VMEM((nu, ts // HG_CHUNK, HG_DK), F32),
                        pltpu.VMEM((nu, ts, HG_DV), F32),
                        pltpu.VMEM((nu, HG_DV, HG_DK), F32)],
        compiler_params=_cparams(("parallel", "arbitrary")),
        name="hgrn2",
    )(proj, proj, proj, proj, lb.reshape(1, HG_QK), out_norm.reshape(1, HG_DV), tmat)


def _t5_band_buckets():
    t = np.arange(WINDOW)[:, None]
    s = np.arange(2 * WINDOW)[None, :]
    dist = t + WINDOW - s
    max_exact = N_BUCKETS // 2
    large = max_exact + (np.log(np.maximum(dist, max_exact) / max_exact) / math.log(MAX_DISTANCE / max_exact)
                         * (N_BUCKETS - max_exact)).astype(np.int32)
    large = np.minimum(large, N_BUCKETS - 1)
    band = (dist >= 0) & (dist < WINDOW)
    return np.where(dist < max_exact, np.maximum(dist, 0), large).astype(np.int32), band


def _swa_kernel(sink_ref, q_ref, kc_ref, kp_ref, vc_ref, vp_ref, bias_ref, qg_ref, kg_ref, gmat_ref, o_ref):
    n = pl.program_id(1)
    w2 = 2 * WINDOW
    no_prev = jnp.where(n > 0, 0.0, -jnp.inf)
    q_groups = SW_HEADS * SW_DIM // LANES
    kv_groups = SW_KV_HEADS * SW_DIM // LANES
    per_kv = q_groups // kv_groups
    kv_rows = (SW_QBLK + 1) * WINDOW
    grp = lambda ref, p: ref[0, :, p * LANES:(p + 1) * LANES]

    def head_norm(x, gain):
        sq = jnp.concatenate(_split_bf16(x * x), axis=1)
        ssq = _dot(sq, gmat_ref[...])
        return x * lax.rsqrt(ssq * (1.0 / SW_DIM) + EPS) * gain

    qn = head_norm(jnp.concatenate([grp(q_ref, p)[i * WINDOW:(i + 1) * WINDOW]
                                    for i in range(SW_QBLK) for p in range(q_groups)], axis=0), qg_ref[...])
    kn = head_norm(jnp.concatenate([x for r in range(kv_groups) for x in (grp(kp_ref, r), grp(kc_ref, r))], axis=0),
                   kg_ref[...])
    v = jnp.concatenate([x for r in range(kv_groups) for x in (grp(vp_ref, r), grp(vc_ref, r))], axis=0)
    kn_rot, v_rot = pltpu.roll(kn, SW_DIM, 1), pltpu.roll(v, SW_DIM, 1)
    kn, kn_rot, v, v_rot = (a.astype(BF16) for a in (kn, kn_rot, v, v_rot))
    ones = jnp.ones((w2, LANES), BF16)

    rows_q = per_kv * WINDOW
    lane = lax.broadcasted_iota(jnp.int32, (rows_q, LANES), 1)
    row = lax.broadcasted_iota(jnp.int32, (rows_q, LANES), 0)
    same_half = (lane < SW_DIM) == (row < rows_q // 2)
    blocks = [(i, r) for i in range(SW_QBLK) for r in range(kv_groups)]
    keys = lambda i, r: slice(r * kv_rows + i * WINDOW, r * kv_rows + i * WINDOW + w2)

    logits = {}
    for i, r in blocks:
        qr = qn[(i * kv_groups + r) * rows_q:(i * kv_groups + r + 1) * rows_q]
        zero = jnp.zeros_like(qr)
        logits[i, r] = (_dot_nt(jnp.where(same_half, qr, zero).astype(BF16), kn[keys(i, r)]),
                        _dot_nt(jnp.where(same_half, zero, qr).astype(BF16), kn_rot[keys(i, r)]))

    weights, sink_terms = {}, {}
    for i, r in blocks:
        w_r, s_r = ([], []), ([], [])
        for j in range(per_kv):
            p = r * per_kv + j
            second = j >= per_kv // 2
            for variant, h in ((0, 2 * p + int(second)), (1, 2 * p + int(not second))):
                lg = logits[i, r][variant][j * WINDOW:(j + 1) * WINDOW]
                bias_prev = bias_ref[h, :, :WINDOW]
                lp = lg[:, :WINDOW] + (bias_prev + no_prev if i == 0 else bias_prev)
                lc = lg[:, WINDOW:] + bias_ref[h, :, WINDOW:]
                sink = sink_ref[h]
                m = jnp.maximum(jnp.max(jnp.maximum(lp, lc), axis=-1, keepdims=True), sink)
                w_r[variant].append(jnp.concatenate([jnp.exp(lp - m), jnp.exp(lc - m)], axis=1).astype(BF16))
                s_r[variant].append(jnp.broadcast_to(jnp.exp(sink - m), (WINDOW, LANES)))
        weights[i, r] = tuple(jnp.concatenate(x, axis=0) for x in w_r)
        sink_terms[i, r] = tuple(jnp.concatenate(x, axis=0) for x in s_r)

    outs = {(i, r): (_dot(weights[i, r][0], jnp.concatenate([v[keys(i, r)], ones], axis=1)),
                     _dot(weights[i, r][1], jnp.concatenate([v_rot[keys(i, r)], ones], axis=1)))
            for i, r in blocks}
    for i, r in blocks:
        o = [outs[i, r][t][:, :LANES] / (outs[i, r][t][:, LANES:] + sink_terms[i, r][t]) for t in range(2)]
        o = jnp.where(same_half, o[0], o[1]).astype(o_ref.dtype)
        for j in range(per_kv):
            p = r * per_kv + j
            o_ref[0, i * WINDOW:(i + 1) * WINDOW, p * LANES:(p + 1) * LANES] = o[j * WINDOW:(j + 1) * WINDOW]


def _swa_group_sum():
    i = np.arange(2 * LANES)[:, None] % LANES
    j = np.arange(LANES)[None, :]
    return (i // SW_DIM == j // SW_DIM).astype(np.float32)


def sliding_window(proj, q_norm, k_norm, sinks, rel_bias):
    b, s, _ = proj.shape
    bucket, band = _t5_band_buckets()
    onehot = jnp.asarray(bucket[..., None] == np.arange(N_BUCKETS), F32)
    bias = jnp.einsum('tsb,bh->hts', onehot, rel_bias.astype(F32), precision=lax.Precision.HIGHEST)
    bias = jnp.where(band, bias, -jnp.inf)
    kv_w = SW_KV_HEADS * SW_DIM
    q_w = SW_HEADS * SW_DIM
    k_blk = q_w // kv_w
    per_lane_group = LANES // SW_DIM
    q_gain = jnp.tile(q_norm.astype(F32), per_lane_group).reshape(1, LANES) * (SW_DIM ** -0.5)
    k_gain = jnp.tile(k_norm.astype(F32), per_lane_group).reshape(1, LANES)
    tq = SW_QBLK * WINDOW
    assert s % tq == 0
    prev = lambda n: jnp.maximum(SW_QBLK * n - 1, 0)
    const2 = lambda i, n: (0, 0)
    return pl.pallas_call(
        _swa_kernel,
        grid=(b, s // tq),
        in_specs=[pl.BlockSpec(memory_space=pltpu.SMEM),
                  pl.BlockSpec((1, tq, q_w), lambda i, n: (i, n, 0)),
                  pl.BlockSpec((1, tq, kv_w), lambda i, n: (i, n, k_blk)),
                  pl.BlockSpec((1, WINDOW, kv_w), lambda i, n: (i, prev(n), k_blk)),
                  pl.BlockSpec((1, tq, kv_w), lambda i, n: (i, n, k_blk + 1)),
                  pl.BlockSpec((1, WINDOW, kv_w), lambda i, n: (i, prev(n), k_blk + 1)),
                  pl.BlockSpec((SW_HEADS, WINDOW, 2 * WINDOW), lambda i, n: (0, 0, 0)),
                  pl.BlockSpec((1, LANES), const2),
                  pl.BlockSpec((1, LANES), const2),
                  pl.BlockSpec((2 * LANES, LANES), const2)],
        out_specs=pl.BlockSpec((1, tq, q_w), lambda i, n: (i, n, 0)),
        out_shape=jax.ShapeDtypeStruct((b, s, q_w), BF16),
        compiler_params=_cparams(("parallel", "parallel")),
        name="sliding_window",
    )(sinks.astype(F32), proj, proj, proj, proj, proj, bias, q_gain, k_gain,
      jnp.asarray(_swa_group_sum(), BF16))


def kernel(x, p, mix_norm, ab_w_in, hg_lb_logits, hg_out_norm, ab_w_out, c_w_in, q_norm, k_norm, sinks,
           rel_bias, c_w_out, ffn_norm, ffn_up, ffn_conv, ffn_conv_b, ffn_down, ple_norm, ple_gate, ple_proj):
    b, s, d = x.shape
    depth = p.shape[0]
    m = b * s
    lb_cum = jnp.cumsum(jax.nn.softmax(hg_lb_logits.astype(F32), axis=0), axis=0)
    lower_bounds = lb_cum - lb_cum[0]
    ab_w_in, ab_w_out, c_w_in, c_w_out, ffn_up, ffn_down, ple_gate, ple_proj = (
        w.astype(BF16) for w in (ab_w_in, ab_w_out, c_w_in, c_w_out, ffn_up, ffn_down, ple_gate, ple_proj))
    conv_b = ffn_conv_b.reshape(depth, 1, -1)
    p = p.reshape(depth, m, -1)

    def in_proj(i):
        if i % 2 == 0:
            return mix_norm[i], ab_w_in, i // 2, (3 * SB_WIDTH, 2 * HG_QK + 2 * HG_V), (BF16, F32)
        return mix_norm[i], c_w_in, i // 2, (c_w_in.shape[2],), (F32,)

    h = x.reshape(m, d)
    proj = norm_proj(h, *in_proj(0))
    for i in range(depth):
        j = i // 2
        if i % 2 == 0:
            att, gates = proj
            o_a = stick_breaking(att.reshape(b, s, -1))
            o_b = hgrn2(gates.reshape(b, s, -1), lower_bounds[j], hg_out_norm[j])
            parts, w_out = [o_a.reshape(m, -1), o_b.reshape(m, -1)], ab_w_out
        else:
            (qkv,) = proj
            o = sliding_window(qkv.reshape(b, s, -1), q_norm[j], k_norm[j], sinks[j], rel_bias)
            parts, w_out = [o.reshape(m, -1)], c_w_out
        h = mix_ffn(parts, w_out, j, h, ffn_norm[i], ffn_up, ffn_conv, conv_b, ffn_down, i, s)
        h, *proj = ple_and_proj(h, ple_norm[i], ple_gate, p, ple_proj, i, in_proj(i + 1) if i + 1 < depth else None)
    return h.reshape(b, s, d)
```

```python
import functools
import math

import numpy as np
import jax
import jax.numpy as jnp
from jax import lax
from jax.experimental import pallas as pl
from jax.experimental.pallas import tpu as pltpu

F32 = jnp.float32
BF16 = jnp.bfloat16

D_MODEL = 1024
PLE_DIM = 256
EPS = 1e-6
SB_HEADS = 8
SB_DIM = 64
SB_WIDTH = SB_HEADS * SB_DIM
HG_HEADS = 4
HG_DK = 128
HG_DV = 128
HG_QK = HG_HEADS * HG_DK
HG_V = HG_HEADS * HG_DV
SW_HEADS = 16
SW_KV_HEADS = 4
SW_DIM = 64
SW_GROUP = SW_HEADS // SW_KV_HEADS
WINDOW = 128
SW_QBLK = 2
N_BUCKETS = 32
MAX_DISTANCE = 128
D_FF = 2816
CONV_W = 3

LANES = 128
SUBLANES = 8
BLK = 128
HG_CHUNK = 16
HG_BATCH = 2
FF_CHUNK = 256
SB_DEAD_LOG2 = -150.0
SB_NEAR_ROWS = 32
SB_QBLK = 2
LOG2E = 1.4426950408889634
VMEM_LIMIT = 56 * 1024 * 1024


def _cparams(sem):
    return pltpu.CompilerParams(dimension_semantics=sem, vmem_limit_bytes=VMEM_LIMIT)


def _rms(x, g):
    ms = jnp.mean(x * x, axis=-1, keepdims=True)
    return x * lax.rsqrt(ms + EPS) * g


def _dot(a, b):
    return jnp.dot(a, b, preferred_element_type=F32)


def _dot_nt(a, b):
    return lax.dot_general(a, b, (((1,), (1,)), ((), ())), preferred_element_type=F32)


def _dot_tn(a, b):
    return lax.dot_general(a, b, (((0,), (0,)), ((), ())), preferred_element_type=F32)


def _split_bf16(x):
    hi = x.astype(BF16)
    lo = (x - hi.astype(F32)).astype(BF16)
    return hi, lo


def _layer_spec(stack, layer, single_buffer=False):
    zeros = (0,) * (stack.ndim - 1)
    kwargs = dict(pipeline_mode=pl.Buffered(1)) if single_buffer else {}
    return pl.BlockSpec((None,) + stack.shape[1:], lambda *_: (layer,) + zeros, **kwargs)


def _row_spec(tm, width):
    return pl.BlockSpec((tm, width), lambda i: (i, 0))


def _project(hn, w_ref, o_refs, splits, nchunk):
    col = 0
    for o_ref, width in zip(o_refs, splits):
        for c0 in range(0, width, nchunk):
            cw = min(nchunk, width - c0)
            o_ref[:, c0:c0 + cw] = _dot(hn, w_ref[:, col + c0:col + c0 + cw]).astype(o_ref.dtype)
        col += width


def _norm_proj_kernel(h_ref, g_ref, w_ref, *o_refs, splits, nchunk):
    _project(_rms(h_ref[...], g_ref[...]).astype(BF16), w_ref, o_refs, splits, nchunk)


def norm_proj(h, g, w_stack, layer, splits, dtypes, tm=512, nchunk=512):
    m, k = h.shape
    assert sum(splits) == w_stack.shape[2] and m % tm == 0
    return pl.pallas_call(
        functools.partial(_norm_proj_kernel, splits=tuple(splits), nchunk=nchunk),
        grid=(m // tm,),
        in_specs=[_row_spec(tm, k), pl.BlockSpec((1, k), lambda i: (0, 0)), _layer_spec(w_stack, layer)],
        out_specs=[_row_spec(tm, s) for s in splits],
        out_shape=[jax.ShapeDtypeStruct((m, s), d) for s, d in zip(splits, dtypes)],
        compiler_params=_cparams(("parallel",)),
        name="norm_proj",
    )(h, g.reshape(1, k), w_stack)


def _mix_ffn_kernel(*refs, n_parts, tiles_per_seq):
    a_refs = refs[:n_parts]
    wo_ref, h_ref, g_ref, wup_ref, cw_ref, cb_ref, wdn_ref, o_ref, hn_ref, act_ref, halo_ref = refs[n_parts:]
    tm = h_ref.shape[0]
    x = h_ref[...]
    k0 = 0
    for a_ref in a_refs:
        k = a_ref.shape[1]
        x = x + _dot(a_ref[...], wo_ref[k0:k0 + k, :])
        k0 += k
    hn_ref[...] = _rms(x, g_ref[...]).astype(BF16)
    seq_start = (pl.program_id(0) % tiles_per_seq) == 0
    row = lax.broadcasted_iota(jnp.int32, (tm, FF_CHUNK), 0)

    def conv_cols(c0):
        cols = slice(c0, c0 + FF_CHUNK)
        u = _dot(hn_ref[...], wup_ref[:, cols])
        prev = jnp.where(seq_start, 0.0, halo_ref[:, cols])
        halo_ref[:, cols] = u[tm - SUBLANES:, :]
        p1 = prev[SUBLANES - 1:SUBLANES, :]
        p2 = prev[SUBLANES - 2:SUBLANES - 1, :]
        u1 = jnp.where(row == 0, p1, pltpu.roll(u, 1, 0))
        u2 = jnp.where(row == 0, p2, jnp.where(row == 1, p1, pltpu.roll(u, 2, 0)))
        cw = cw_ref[:, cols]
        return cw[0:1, :] * u2 + cw[1:2, :] * u1 + cw[2:3, :] * u + cb_ref[:, cols]

    for c in range(D_FF // FF_CHUNK):
        gate = conv_cols(c * FF_CHUNK)
        up = conv_cols(D_FF + c * FF_CHUNK)
        act_ref[:, c * FF_CHUNK:(c + 1) * FF_CHUNK] = (gate * jax.nn.sigmoid(gate) * up).astype(BF16)
    o_ref[...] = x + _dot(act_ref[...], wdn_ref[...])


def mix_ffn(parts, w_out, out_layer, h, g, w_up, conv_w, conv_b, w_down, layer, seq, tm=1024):
    m, d = h.shape
    f2 = w_up.shape[2]
    assert seq % tm == 0 and D_FF % FF_CHUNK == 0 and sum(a.shape[1] for a in parts) == w_out.shape[1]
    return pl.pallas_call(
        functools.partial(_mix_ffn_kernel, n_parts=len(parts), tiles_per_seq=seq // tm),
        grid=(m // tm,),
        in_specs=[_row_spec(tm, a.shape[1]) for a in parts]
                 + [_layer_spec(w_out, out_layer, True), _row_spec(tm, d), pl.BlockSpec((1, d), lambda i: (0, 0)),
                    _layer_spec(w_up, layer, True), _layer_spec(conv_w, layer), _layer_spec(conv_b, layer),
                    _layer_spec(w_down, layer, True)],
        out_specs=_row_spec(tm, d),
        out_shape=jax.ShapeDtypeStruct((m, d), F32),
        scratch_shapes=[pltpu.VMEM((tm, d), BF16),
                        pltpu.VMEM((tm, D_FF), BF16),
                        pltpu.VMEM((SUBLANES, f2), F32)],
        compiler_params=_cparams(("arbitrary",)),
        name="mix_ffn",
    )(*parts, w_out, h, g.reshape(1, d), w_up, conv_w, conv_b, w_down)


def _ple_proj_kernel(*refs, splits, nchunk):
    h_ref, g_ref, wg_ref, p_ref, wp_ref = refs[:5]
    x = h_ref[...]
    gate = jax.nn.sigmoid(_dot(_rms(x, g_ref[...]).astype(BF16), wg_ref[...]))
    x = x + gate * _dot(p_ref[...].astype(BF16), wp_ref[...])
    if splits:
        g2_ref, win_ref, o_ref, *proj_refs = refs[5:]
        _project(_rms(x, g2_ref[...]).astype(BF16), win_ref, proj_refs, splits, nchunk)
    else:
        (o_ref,) = refs[5:]
    o_ref[...] = x


def ple_and_proj(h, g, w_gate, p, w_proj, layer, nxt=None, tm=512, nchunk=512):
    m, d = h.shape
    in_specs = [_row_spec(tm, d), pl.BlockSpec((1, d), lambda i: (0, 0)), _layer_spec(w_gate, layer, True),
                pl.BlockSpec((None, tm, p.shape[2]), lambda i: (layer, i, 0)), _layer_spec(w_proj, layer, True)]
    args = [h, g.reshape(1, d), w_gate, p, w_proj]
    out_specs = [_row_spec(tm, d)]
    out_shape = [jax.ShapeDtypeStruct((m, d), F32)]
    splits = ()
    if nxt is not None:
        g2, w_in, in_layer, splits, dtypes = nxt
        assert sum(splits) == w_in.shape[2]
        in_specs += [pl.BlockSpec((1, d), lambda i: (0, 0)), _layer_spec(w_in, in_layer, True)]
        args += [g2.reshape(1, d), w_in]
        out_specs += [_row_spec(tm, s) for s in splits]
        out_shape += [jax.ShapeDtypeStruct((m, s), dt) for s, dt in zip(splits, dtypes)]
    return pl.pallas_call(
        functools.partial(_ple_proj_kernel, splits=tuple(splits), nchunk=nchunk),
        grid=(m // tm,),
        in_specs=in_specs,
        out_specs=out_specs,
        out_shape=out_shape,
        compiler_params=_cparams(("parallel",)),
        name="ple_proj",
    )(*args)


def _sb_tri():
    j = np.arange(2 * BLK)[:, None] % BLK
    s = np.arange(2 * BLK)[None, :]
    return ((s >= BLK) | (j >= s)).astype(np.float32)


def _neg_abs(x):
    sign = jnp.uint32(0x80000000)
    return lax.bitcast_convert_type(lax.bitcast_convert_type(x, jnp.uint32) | sign, F32)


def _sb_kernel(q_ref, k_ref, v_ref, tri_ref, o_ref, qs_ref, acc_ref, c_ref):
    n0 = SB_QBLK * pl.program_id(1)
    npair = SB_WIDTH // LANES
    pairs = range(npair)
    cols = [slice(p * LANES, (p + 1) * LANES) for p in pairs]
    lane = lax.broadcasted_iota(jnp.int32, (2 * BLK, BLK), 1)
    row = lax.broadcasted_iota(jnp.int32, (2 * BLK, BLK), 0)
    first_head = row < BLK
    causal = lane < jnp.where(first_head, row, row - BLK)
    own_lanes = first_head == (lane < SB_DIM)
    rn = SB_NEAR_ROWS

    for qb in range(SB_QBLK):
        for p in pairs:
            q2 = q_ref[0, qb * BLK:(qb + 1) * BLK, cols[p]] * (SB_DIM ** -0.5)
            q2 = jnp.concatenate([q2, q2], axis=0)
            qs_ref[qb, p] = jnp.where(own_lanes, q2, jnp.zeros_like(q2))

    def stacked(ref, qb, p, lo, hi):
        if (lo, hi) == (0, BLK):
            return ref[qb, p]
        return jnp.concatenate([ref[qb, p, lo:hi], ref[qb, p, BLK + lo:BLK + hi]], axis=0)

    def run(jobs):
        for job in jobs:
            job["krows"] = pl.ds(pl.multiple_of(job["kb"] * BLK, BLK), job["nblk"] * BLK)
            job["tn"] = [_dot_nt(stacked(qs_ref, job["qb"], p, job["lo"], job["hi"]),
                                 k_ref[0, job["krows"], cols[p]]) * (-LOG2E) for p in pairs]
        for job in jobs:
            job["split"] = {}
            for p in pairs:
                for j in range(job["nblk"]):
                    t = job["tn"][p][:, j * BLK:(j + 1) * BLK]
                    log_keep = jnp.minimum(t, 0.0) - jnp.log2(1.0 + jnp.exp2(_neg_abs(t)))
                    if job["diag"] and j == job["nblk"] - 1:
                        log_keep = jnp.where(causal, log_keep, 0.0)
                    job["split"][p, j] = jnp.concatenate(_split_bf16(log_keep), axis=1)
        for job in jobs:
            job["sums"] = {key: _dot(val, tri_ref[...]) for key, val in job["split"].items()}
        for job in jobs:
            qb, lo, hi, nblk = job["qb"], job["lo"], job["hi"], job["nblk"]
            r = hi - lo
            job["ws"], job["c"] = [], []
            for p in pairs:
                if job["diag"]:
                    c = None
                elif job["after"] is not None:
                    prev = jobs[job["after"]]["c"][p]
                    c = jnp.concatenate([prev[lo:hi], prev[BLK + lo:BLK + hi]], axis=0)
                else:
                    c = stacked(c_ref, qb, p, lo, hi)
                w = [None] * nblk
                for j in range(nblk - 1, -1, -1):
                    s = job["sums"][p, j]
                    incl, total = s[:, :BLK], s[:, BLK:]
                    arg = incl - job["tn"][p][:, j * BLK:(j + 1) * BLK]
                    if c is not None:
                        arg = arg + c
                    wj = jnp.exp2(arg)
                    if job["diag"] and j == nblk - 1:
                        wj = jnp.where(causal, wj, 0.0)
                    w[j] = wj.astype(BF16)
                    c = total if c is None else c + total
                job["ws"].append(w[0] if nblk == 1 else jnp.concatenate(w, axis=1))
                job["c"].append(c)
                if (lo, hi) == (0, BLK):
                    c_ref[qb, p] = c
                else:
                    c_ref[qb, p, lo:hi] = c[:r]
                    c_ref[qb, p, BLK + lo:BLK + hi] = c[r:]
        for job in jobs:
            job["pv"] = [_dot(job["ws"][p], v_ref[0, job["krows"], cols[p]]) for p in pairs]
        for job in jobs:
            qb, lo, hi = job["qb"], job["lo"], job["hi"]
            r = hi - lo
            first_half = lax.broadcasted_iota(jnp.int32, (r, BLK), 1) < SB_DIM
            for p in pairs:
                pv = jnp.where(first_half, job["pv"][p][:r], job["pv"][p][r:])
                if job["diag"]:
                    acc_ref[qb, p] = pv
                else:
                    acc_ref[qb, p, lo:hi] += pv

    def job(qb, kb, nblk, diag, lo=0, hi=BLK, after=None):
        return dict(qb=qb, kb=kb, nblk=nblk, diag=diag, lo=lo, hi=hi, after=after)

    def live(qb, lo, hi):
        c = c_ref[qb]
        return jnp.max(jnp.maximum(c[:, lo:hi], c[:, BLK + lo:BLK + hi])) > SB_DEAD_LOG2

    more = lambda st: (st[0] >= 0) & st[1]

    def rest(qb, kb):
        @pl.when(live(qb, rn, BLK))
        def _():
            run([job(qb, kb, 1, False, rn, BLK)])

        def full_block(state):
            run([job(qb, state[0], 1, False)])
            return state[0] - 1, live(qb, rn, BLK)

        def near_block(state):
            run([job(qb, state[0], 1, False, 0, rn)])
            return state[0] - 1, live(qb, 0, rn)

        kb2, _ = lax.while_loop(more, full_block, (kb - 1, live(qb, rn, BLK)))
        lax.while_loop(more, near_block, (kb2, live(qb, 0, rn)))

    @pl.when(n0 == 0)
    def _():
        run([job(qb, 0, qb + 1, True) for qb in range(SB_QBLK)])

    @pl.when(n0 > 0)
    def _():
        jobs = []
        for qb in range(SB_QBLK):
            jobs.append(job(qb, n0 + qb - 1, 2, True))
            jobs.append(job(qb, n0 + qb - 2, 1, False, 0, rn, after=len(jobs) - 1))
        run(jobs)

        @pl.when(jnp.max(c_ref[...]) > SB_DEAD_LOG2)
        def _():
            for qb in range(SB_QBLK):
                rest(qb, n0 + qb - 2)

    for qb in range(SB_QBLK):
        for p in pairs:
            o_ref[0, qb * BLK:(qb + 1) * BLK, cols[p]] = acc_ref[qb, p].astype(o_ref.dtype)


def stick_breaking(proj):
    b, s, _ = proj.shape
    tri = jnp.asarray(_sb_tri(), BF16)
    npair = SB_WIDTH // LANES
    tq = SB_QBLK * BLK
    assert s % tq == 0 and SB_QBLK == 2
    return pl.pallas_call(
        _sb_kernel,
        grid=(b, s // tq),
        in_specs=[pl.BlockSpec((1, tq, SB_WIDTH), lambda i, n: (i, n, 0)),
                  pl.BlockSpec((1, s, SB_WIDTH), lambda i, n: (i, 0, 1)),
                  pl.BlockSpec((1, s, SB_WIDTH), lambda i, n: (i, 0, 2)),
                  pl.BlockSpec((2 * BLK, 2 * BLK), lambda i, n: (0, 0))],
        out_specs=pl.BlockSpec((1, tq, SB_WIDTH), lambda i, n: (i, n, 0)),
        out_shape=jax.ShapeDtypeStruct((b, s, SB_WIDTH), BF16),
        scratch_shapes=[pltpu.VMEM((SB_QBLK, npair, 2 * BLK, LANES), BF16),
                        pltpu.VMEM((SB_QBLK, npair, BLK, LANES), F32),
                        pltpu.VMEM((SB_QBLK, npair, 2 * BLK, LANES), F32)],
        compiler_params=_cparams(("parallel", "arbitrary")),
        name="stick_breaking",
    )(proj, proj, proj, tri)


def _hg_time_mats():
    t = np.arange(BLK)[:, None]
    j = np.arange(BLK)[None, :]
    same = (t // HG_CHUNK) == (j // HG_CHUNK)
    per_chunk = np.arange(2 * SUBLANES)[:, None] == (j // HG_CHUNK)
    return np.concatenate([same & (j <= t), same, per_chunk], axis=0).astype(np.float32)


def _hgrn_kernel(q_ref, f_ref, i_ref, g_ref, lb_ref, gn_ref, tmat_ref, o_ref,
                 qf_s, kk_s, b_s, iv_s, qt_s, kt_s, dec_s, raw_s, state_s):
    ts = q_ref.shape[1]
    units = [(bi, h) for bi in range(q_ref.shape[0]) for h in range(HG_HEADS)]
    hcols = [slice(h * LANES, (h + 1) * LANES) for h in range(HG_HEADS)]
    cpb = BLK // HG_CHUNK

    @pl.when(pl.program_id(1) == 0)
    def _():
        state_s[...] = jnp.zeros_like(state_s)

    def prep(r, _):
        rows = pl.ds(pl.multiple_of(r * BLK, BLK), BLK)
        for u, (bi, h) in enumerate(units):
            lb = lb_ref[:, hcols[h]]
            fp = f_ref[bi, rows, hcols[h]]
            q = q_ref[bi, rows, hcols[h]]
            e = jnp.exp(-jnp.abs(fp))
            sg = 1.0 / (1.0 + e)
            log_f = jnp.log(lb + (1.0 - lb) * jnp.where(fp >= 0, sg, e * sg))
            kk = (1.0 - lb) * jnp.where(fp >= 0, e * sg, sg)
            qf = q * jax.nn.sigmoid(q)
            hi, lo = _split_bf16(log_f)
            sums = _dot(tmat_ref[...], jnp.concatenate([hi, lo], axis=1))
            sums = sums[:, :LANES] + sums[:, LANES:]
            b = sums[:BLK]
            b_tot = sums[BLK:2 * BLK]
            chunk_tot = sums[2 * BLK:2 * BLK + cpb]
            qf_s[u, rows, :] = qf
            kk_s[u, rows, :] = kk
            b_s[u, rows, :] = b
            iv_s[u, rows, :] = i_ref[bi, rows, hcols[h]]
            qt_s[u, rows, :] = (qf * jnp.exp(b)).astype(BF16)
            kt_s[u, rows, :] = (kk * jnp.exp(b_tot - b)).astype(BF16)
            dec_s[u, pl.ds(pl.multiple_of(r * cpb, cpb), cpb), :] = jnp.exp(chunk_tot)
        return 0

    lax.fori_loop(0, ts // BLK, prep, 0)

    rowi = lax.broadcasted_iota(jnp.int32, (SUBLANES, 1), 0)

    def chunk(c, _):
        r0 = pl.multiple_of(c * HG_CHUNK, HG_CHUNK)
        rows = pl.ds(r0, HG_CHUNK)
        inter = [_dot_nt(qt_s[u, rows, :], state_s[u].astype(BF16)) for u in range(len(units))]
        for u in range(len(units)):
            for g in range(HG_CHUNK // SUBLANES):
                g0 = g * SUBLANES
                grows = pl.ds(r0 + g0, SUBLANES)
                qf, b = qf_s[u, grows, :], b_s[u, grows, :]
                o = jnp.zeros((SUBLANES, LANES), F32)
                for s in range(g0 + SUBLANES):
                    one = pl.ds(r0 + s, 1)
                    b_key, k_key, v_key = b_s[u, one, :], kk_s[u, one, :], iv_s[u, one, :]
                    score = jnp.sum(qf * k_key * jnp.exp(b - b_key), axis=-1, keepdims=True)
                    if s > g0:
                        score = jnp.where(rowi >= s - g0, score, 0.0)
                    o = o + score * v_key
                raw_s[u, grows, :] = o + inter[u][g0:g0 + SUBLANES]
        upd = [_dot_tn(iv_s[u, rows, :].astype(BF16), kt_s[u, rows, :]) for u in range(len(units))]
        for u in range(len(units)):
            state_s[u] = state_s[u] * dec_s[u, pl.ds(c, 1), :] + upd[u]
        return 0

    lax.fori_loop(0, ts // HG_CHUNK, chunk, 0, unroll=2)

    def finish(r, _):
        rows = pl.ds(pl.multiple_of(r * BLK, BLK), BLK)
        for u, (bi, h) in enumerate(units):
            g = g_ref[bi, rows, hcols[h]]
            y = _rms(raw_s[u, rows, :], gn_ref[...]) * (g * jax.nn.sigmoid(g))
            o_ref[bi, rows, hcols[h]] = y.astype(o_ref.dtype)
        return 0

    lax.fori_loop(0, ts // BLK, finish, 0)


def hgrn2(proj, lb, out_norm, ts=512):
    b, s, _ = proj.shape
    assert s % ts == 0 and ts % BLK == 0 and b % HG_BATCH == 0
    tmat = jnp.asarray(_hg_time_mats(), BF16)
    seq_blk = lambda part: pl.BlockSpec((HG_BATCH, ts, HG_QK), lambda i, t: (i, t, part))
    const = lambda i, t: (0, 0)
    nu = HG_BATCH * HG_HEADS
    return pl.pallas_call(
        _hgrn_kernel,
        grid=(b // HG_BATCH, s // ts),
        in_specs=[seq_blk(0), seq_blk(1), seq_blk(2), seq_blk(3),
                  pl.BlockSpec((1, HG_QK), const),
                  pl.BlockSpec((1, HG_DV), const),
                  pl.BlockSpec(tmat.shape, const)],
        out_specs=pl.BlockSpec((HG_BATCH, ts, HG_V), lambda i, t: (i, t, 0)),
        out_shape=jax.ShapeDtypeStruct((b, s, HG_V), BF16),
        scratch_shapes=[pltpu.VMEM((nu, ts, HG_DK), F32), pltpu.VMEM((nu, ts, HG_DK), F32),
                        pltpu.VMEM((nu, ts, HG_DK), F32), pltpu.VMEM((nu, ts, HG_DV), F32),
                        pltpu.VMEM((nu, ts, HG_DK), BF16), pltpu.VMEM((nu, ts, HG_DK), BF16),
                        pltpu.VMEM((nu, ts // HG_CHUNK, HG_DK), F32),
                        pltpu.VMEM((nu, ts, HG_DV), F32),
                        pltpu.VMEM((nu, HG_DV, HG_DK), F32)],
        compiler_params=_cparams(("parallel", "arbitrary")),
        name="hgrn2",
    )(proj, proj, proj, proj, lb.reshape(1, HG_QK), out_norm.reshape(1, HG_DV), tmat)


def _t5_band_buckets():
    t = np.arange(WINDOW)[:, None]
    s = np.arange(2 * WINDOW)[None, :]
    dist = t + WINDOW - s
    max_exact = N_BUCKETS // 2
    large = max_exact + (np.log(np.maximum(dist, max_exact) / max_exact) / math.log(MAX_DISTANCE / max_exact)
                         * (N_BUCKETS - max_exact)).astype(np.int32)
    large = np.minimum(large, N_BUCKETS - 1)
    band = (dist >= 0) & (dist < WINDOW)
    return np.where(dist < max_exact, np.maximum(dist, 0), large).astype(np.int32), band


def _swa_kernel(sink_ref, q_ref, kc_ref, kp_ref, vc_ref, vp_ref, bias_ref, qg_ref, kg_ref, gmat_ref, o_ref):
    n = pl.program_id(1)
    w2 = 2 * WINDOW
    no_prev = jnp.where(n > 0, 0.0, -jnp.inf)
    q_groups = SW_HEADS * SW_DIM // LANES
    kv_groups = SW_KV_HEADS * SW_DIM // LANES
    per_kv = q_groups // kv_groups
    kv_rows = (SW_QBLK + 1) * WINDOW
    grp = lambda ref, p: ref[0, :, p * LANES:(p + 1) * LANES]

    def head_norm(x, gain):
        sq = jnp.concatenate(_split_bf16(x * x), axis=1)
        ssq = _dot(sq, gmat_ref[...])
        return x * lax.rsqrt(ssq * (1.0 / SW_DIM) + EPS) * gain

    qn = head_norm(jnp.concatenate([grp(q_ref, p)[i * WINDOW:(i + 1) * WINDOW]
                                    for i in range(SW_QBLK) for p in range(q_groups)], axis=0), qg_ref[...])
    kn = head_norm(jnp.concatenate([x for r in range(kv_groups) for x in (grp(kp_ref, r), grp(kc_ref, r))], axis=0),
                   kg_ref[...])
    v = jnp.concatenate([x for r in range(kv_groups) for x in (grp(vp_ref, r), grp(vc_ref, r))], axis=0)
    kn_rot, v_rot = pltpu.roll(kn, SW_DIM, 1), pltpu.roll(v, SW_DIM, 1)
    kn, kn_rot, v, v_rot = (a.astype(BF16) for a in (kn, kn_rot, v, v_rot))
    ones = jnp.ones((w2, LANES), BF16)

    rows_q = per_kv * WINDOW
    lane = lax.broadcasted_iota(jnp.int32, (rows_q, LANES), 1)
    row = lax.broadcasted_iota(jnp.int32, (rows_q, LANES), 0)
    same_half = (lane < SW_DIM) == (row < rows_q // 2)
    blocks = [(i, r) for i in range(SW_QBLK) for r in range(kv_groups)]
    keys = lambda i, r: slice(r * kv_rows + i * WINDOW, r * kv_rows + i * WINDOW + w2)

    logits = {}
    for i, r in blocks:
        qr = qn[(i * kv_groups + r) * rows_q:(i * kv_groups + r + 1) * rows_q]
        zero = jnp.zeros_like(qr)
        logits[i, r] = (_dot_nt(jnp.where(same_half, qr, zero).astype(BF16), kn[keys(i, r)]),
                        _dot_nt(jnp.where(same_half, zero, qr).astype(BF16), kn_rot[keys(i, r)]))

    weights, sink_terms = {}, {}
    for i, r in blocks:
        w_r, s_r = ([], []), ([], [])
        for j in range(per_kv):
            p = r * per_kv + j
            second = j >= per_kv // 2
            for variant, h in ((0, 2 * p + int(second)), (1, 2 * p + int(not second))):
                lg = logits[i, r][variant][j * WINDOW:(j + 1) * WINDOW]
                bias_prev = bias_ref[h, :, :WINDOW]
                lp = lg[:, :WINDOW] + (bias_prev + no_prev if i == 0 else bias_prev)
                lc = lg[:, WINDOW:] + bias_ref[h, :, WINDOW:]
                sink = sink_ref[h]
                m = jnp.maximum(jnp.max(jnp.maximum(lp, lc), axis=-1, keepdims=True), sink)
                w_r[variant].append(jnp.concatenate([jnp.exp(lp - m), jnp.exp(lc - m)], axis=1).astype(BF16))
                s_r[variant].append(jnp.broadcast_to(jnp.exp(sink - m), (WINDOW, LANES)))
        weights[i, r] = tuple(jnp.concatenate(x, axis=0) for x in w_r)
        sink_terms[i, r] = tuple(jnp.concatenate(x, axis=0) for x in s_r)

    outs = {(i, r): (_dot(weights[i, r][0], jnp.concatenate([v[keys(i, r)], ones], axis=1)),
                     _dot(weights[i, r][1], jnp.concatenate([v_rot[keys(i, r)], ones], axis=1)))
            for i, r in blocks}
    for i, r in blocks:
        o = [outs[i, r][t][:, :LANES] / (outs[i, r][t][:, LANES:] + sink_terms[i, r][t]) for t in range(2)]
        o = jnp.where(same_half, o[0], o[1]).astype(o_ref.dtype)
        for j in range(per_kv):
            p = r * per_kv + j
            o_ref[0, i * WINDOW:(i + 1) * WINDOW, p * LANES:(p + 1) * LANES] = o[j * WINDOW:(j + 1) * WINDOW]


def _swa_group_sum():
    i = np.arange(2 * LANES)[:, None] % LANES
    j = np.arange(LANES)[None, :]
    return (i // SW_DIM == j // SW_DIM).astype(np.float32)


def sliding_window(proj, q_norm, k_norm, sinks, rel_bias):
    b, s, _ = proj.shape
    bucket, band = _t5_band_buckets()
    onehot = jnp.asarray(bucket[..., None] == np.arange(N_BUCKETS), F32)
    bias = jnp.einsum('tsb,bh->hts', onehot, rel_bias.astype(F32), precision=lax.Precision.HIGHEST)
    bias = jnp.where(band, bias, -jnp.inf)
    kv_w = SW_KV_HEADS * SW_DIM
    q_w = SW_HEADS * SW_DIM
    k_blk = q_w // kv_w
    per_lane_group = LANES // SW_DIM
    q_gain = jnp.tile(q_norm.astype(F32), per_lane_group).reshape(1, LANES) * (SW_DIM ** -0.5)
    k_gain = jnp.tile(k_norm.astype(F32), per_lane_group).reshape(1, LANES)
    tq = SW_QBLK * WINDOW
    assert s % tq == 0
    prev = lambda n: jnp.maximum(SW_QBLK * n - 1, 0)
    const2 = lambda i, n: (0, 0)
    return pl.pallas_call(
        _swa_kernel,
        grid=(b, s // tq),
        in_specs=[pl.BlockSpec(memory_space=pltpu.SMEM),
                  pl.BlockSpec((1, tq, q_w), lambda i, n: (i, n, 0)),
                  pl.BlockSpec((1, tq, kv_w), lambda i, n: (i, n, k_blk)),
                  pl.BlockSpec((1, WINDOW, kv_w), lambda i, n: (i, prev(n), k_blk)),
                  pl.BlockSpec((1, tq, kv_w), lambda i, n: (i, n, k_blk + 1)),
                  pl.BlockSpec((1, WINDOW, kv_w), lambda i, n: (i, prev(n), k_blk + 1)),
                  pl.BlockSpec((SW_HEADS, WINDOW, 2 * WINDOW), lambda i, n: (0, 0, 0)),
                  pl.BlockSpec((1, LANES), const2),
                  pl.BlockSpec((1, LANES), const2),
                  pl.BlockSpec((2 * LANES, LANES), const2)],
        out_specs=pl.BlockSpec((1, tq, q_w), lambda i, n: (i, n, 0)),
        out_shape=jax.ShapeDtypeStruct((b, s, q_w), BF16),
        compiler_params=_cparams(("parallel", "parallel")),
        name="sliding_window",
    )(sinks.astype(F32), proj, proj, proj, proj, proj, bias, q_gain, k_gain,
      jnp.asarray(_swa_group_sum(), BF16))


def kernel(x, p, mix_norm, ab_w_in, hg_lb_logits, hg_out_norm, ab_w_out, c_w_in, q_norm, k_norm, sinks,
           rel_bias, c_w_out, ffn_norm, ffn_up, ffn_conv, ffn_conv_b, ffn_down, ple_norm, ple_gate, ple_proj):
    b, s, d = x.shape
    depth = p.shape[0]
    m = b * s
    lb_cum = jnp.cumsum(jax.nn.softmax(hg_lb_logits.astype(F32), axis=0), axis=0)
    lower_bounds = lb_cum - lb_cum[0]
    ab_w_in, ab_w_out, c_w_in, c_w_out, ffn_up, ffn_down, ple_gate, ple_proj = (
        w.astype(BF16) for w in (ab_w_in, ab_w_out, c_w_in, c_w_out, ffn_up, ffn_down, ple_gate, ple_proj))
    conv_b = ffn_conv_b.reshape(depth, 1, -1)
    p = p.reshape(depth, m, -1)

    def in_proj(i):
        if i % 2 == 0:
            return mix_norm[i], ab_w_in, i // 2, (3 * SB_WIDTH, 2 * HG_QK + 2 * HG_V), (BF16, F32)
        return mix_norm[i], c_w_in, i // 2, (c_w_in.shape[2],), (F32,)

    h = x.reshape(m, d)
    proj = norm_proj(h, *in_proj(0))
    for i in range(depth):
        j = i // 2
        if i % 2 == 0:
            att, gates = proj
            o_a = stick_breaking(att.reshape(b, s, -1))
            o_b = hgrn2(gates.reshape(b, s, -1), lower_bounds[j], hg_out_norm[j])
            parts, w_out = [o_a.reshape(m, -1), o_b.reshape(m, -1)], ab_w_out
        else:
            (qkv,) = proj
            o = sliding_window(qkv.reshape(b, s, -1), q_norm[j], k_norm[j], sinks[j], rel_bias)
            parts, w_out = [o.reshape(m, -1)], c_w_out
        h = mix_ffn(parts, w_out, j, h, ffn_norm[i], ffn_up, ffn_conv, conv_b, ffn_down, i, s)
        h, *proj = ple_and_proj(h, ple_norm[i], ple_gate, p, ple_proj, i, in_proj(i + 1) if i + 1 < depth else None)
    return h.reshape(b, s, d)
```

```python
import functools
import math

import numpy as np
import jax
import jax.numpy as jnp
from jax import lax
from jax.experimental import pallas as pl
from jax.experimental.pallas import tpu as pltpu

F32 = jnp.float32
BF16 = jnp.bfloat16

D_MODEL = 1024
PLE_DIM = 256
EPS = 1e-6
SB_HEADS = 8
SB_DIM = 64
SB_WIDTH = SB_HEADS * SB_DIM
HG_HEADS = 4
HG_DK = 128
HG_DV = 128
HG_QK = HG_HEADS * HG_DK
HG_V = HG_HEADS * HG_DV
SW_HEADS = 16
SW_KV_HEADS = 4
SW_DIM = 64
SW_GROUP = SW_HEADS // SW_KV_HEADS
WINDOW = 128
SW_QBLK = 2
N_BUCKETS = 32
MAX_DISTANCE = 128
D_FF = 2816
CONV_W = 3

LANES = 128
SUBLANES = 8
BLK = 128
HG_CHUNK = 16
HG_BATCH = 2
FF_CHUNK = 256
SB_DEAD_LOG2 = -150.0
SB_NEAR_ROWS = 32
SB_QBLK = 4
LOG2E = 1.4426950408889634
VMEM_LIMIT = 56 * 1024 * 1024


def _cparams(sem):
    return pltpu.CompilerParams(dimension_semantics=sem, vmem_limit_bytes=VMEM_LIMIT)


def _rms(x, g):
    ms = jnp.mean(x * x, axis=-1, keepdims=True)
    return x * lax.rsqrt(ms + EPS) * g


def _dot(a, b):
    return jnp.dot(a, b, preferred_element_type=F32)


def _dot_nt(a, b):
    return lax.dot_general(a, b, (((1,), (1,)), ((), ())), preferred_element_type=F32)


def _dot_tn(a, b):
    return lax.dot_general(a, b, (((0,), (0,)), ((), ())), preferred_element_type=F32)


def _split_bf16(x):
    hi = x.astype(BF16)
    lo = (x - hi.astype(F32)).astype(BF16)
    return hi, lo


def _layer_spec(stack, layer, single_buffer=False):
    zeros = (0,) * (stack.ndim - 1)
    kwargs = dict(pipeline_mode=pl.Buffered(1)) if single_buffer else {}
    return pl.BlockSpec((None,) + stack.shape[1:], lambda *_: (layer,) + zeros, **kwargs)


def _row_spec(tm, width):
    return pl.BlockSpec((tm, width), lambda i: (i, 0))


def _project(hn, w_ref, o_refs, splits, nchunk):
    col = 0
    for o_ref, width in zip(o_refs, splits):
        for c0 in range(0, width, nchunk):
            cw = min(nchunk, width - c0)
            o_ref[:, c0:c0 + cw] = _dot(hn, w_ref[:, col + c0:col + c0 + cw]).astype(o_ref.dtype)
        col += width


def _norm_proj_kernel(h_ref, g_ref, w_ref, *o_refs, splits, nchunk):
    _project(_rms(h_ref[...], g_ref[...]).astype(BF16), w_ref, o_refs, splits, nchunk)


def norm_proj(h, g, w_stack, layer, splits, dtypes, tm=512, nchunk=512):
    m, k = h.shape
    assert sum(splits) == w_stack.shape[2] and m % tm == 0
    return pl.pallas_call(
        functools.partial(_norm_proj_kernel, splits=tuple(splits), nchunk=nchunk),
        grid=(m // tm,),
        in_specs=[_row_spec(tm, k), pl.BlockSpec((1, k), lambda i: (0, 0)), _layer_spec(w_stack, layer)],
        out_specs=[_row_spec(tm, s) for s in splits],
        out_shape=[jax.ShapeDtypeStruct((m, s), d) for s, d in zip(splits, dtypes)],
        compiler_params=_cparams(("parallel",)),
        name="norm_proj",
    )(h, g.reshape(1, k), w_stack)


def _mix_ffn_kernel(*refs, n_parts, tiles_per_seq):
    a_refs = refs[:n_parts]
    wo_ref, h_ref, g_ref, wup_ref, cw_ref, cb_ref, wdn_ref, o_ref, hn_ref, act_ref, halo_ref = refs[n_parts:]
    tm = h_ref.shape[0]
    x = h_ref[...]
    k0 = 0
    for a_ref in a_refs:
        k = a_ref.shape[1]
        x = x + _dot(a_ref[...], wo_ref[k0:k0 + k, :])
        k0 += k
    hn_ref[...] = _rms(x, g_ref[...]).astype(BF16)
    seq_start = (pl.program_id(0) % tiles_per_seq) == 0
    row = lax.broadcasted_iota(jnp.int32, (tm, FF_CHUNK), 0)

    def conv_cols(c0):
        cols = slice(c0, c0 + FF_CHUNK)
        u = _dot(hn_ref[...], wup_ref[:, cols])
        prev = jnp.where(seq_start, 0.0, halo_ref[:, cols])
        halo_ref[:, cols] = u[tm - SUBLANES:, :]
        p1 = prev[SUBLANES - 1:SUBLANES, :]
        p2 = prev[SUBLANES - 2:SUBLANES - 1, :]
        u1 = jnp.where(row == 0, p1, pltpu.roll(u, 1, 0))
        u2 = jnp.where(row == 0, p2, jnp.where(row == 1, p1, pltpu.roll(u, 2, 0)))
        cw = cw_ref[:, cols]
        return cw[0:1, :] * u2 + cw[1:2, :] * u1 + cw[2:3, :] * u + cb_ref[:, cols]

    for c in range(D_FF // FF_CHUNK):
        gate = conv_cols(c * FF_CHUNK)
        up = conv_cols(D_FF + c * FF_CHUNK)
        act_ref[:, c * FF_CHUNK:(c + 1) * FF_CHUNK] = (gate * jax.nn.sigmoid(gate) * up).astype(BF16)
    o_ref[...] = x + _dot(act_ref[...], wdn_ref[...])


def mix_ffn(parts, w_out, out_layer, h, g, w_up, conv_w, conv_b, w_down, layer, seq, tm=1024):
    m, d = h.shape
    f2 = w_up.shape[2]
    assert seq % tm == 0 and D_FF % FF_CHUNK == 0 and sum(a.shape[1] for a in parts) == w_out.shape[1]
    return pl.pallas_call(
        functools.partial(_mix_ffn_kernel, n_parts=len(parts), tiles_per_seq=seq // tm),
        grid=(m // tm,),
        in_specs=[_row_spec(tm, a.shape[1]) for a in parts]
                 + [_layer_spec(w_out, out_layer, True), _row_spec(tm, d), pl.BlockSpec((1, d), lambda i: (0, 0)),
                    _layer_spec(w_up, layer, True), _layer_spec(conv_w, layer), _layer_spec(conv_b, layer),
                    _layer_spec(w_down, layer, True)],
        out_specs=_row_spec(tm, d),
        out_shape=jax.ShapeDtypeStruct((m, d), F32),
        scratch_shapes=[pltpu.VMEM((tm, d), BF16),
                        pltpu.VMEM((tm, D_FF), BF16),
                        pltpu.VMEM((SUBLANES, f2), F32)],
        compiler_params=_cparams(("arbitrary",)),
        name="mix_ffn",
    )(*parts, w_out, h, g.reshape(1, d), w_up, conv_w, conv_b, w_down)


def _ple_proj_kernel(*refs, splits, nchunk):
    h_ref, g_ref, wg_ref, p_ref, wp_ref = refs[:5]
    x = h_ref[...]
    gate = jax.nn.sigmoid(_dot(_rms(x, g_ref[...]).astype(BF16), wg_ref[...]))
    x = x + gate * _dot(p_ref[...].astype(BF16), wp_ref[...])
    if splits:
        g2_ref, win_ref, o_ref, *proj_refs = refs[5:]
        _project(_rms(x, g2_ref[...]).astype(BF16), win_ref, proj_refs, splits, nchunk)
    else:
        (o_ref,) = refs[5:]
    o_ref[...] = x


def ple_and_proj(h, g, w_gate, p, w_proj, layer, nxt=None, tm=512, nchunk=512):
    m, d = h.shape
    in_specs = [_row_spec(tm, d), pl.BlockSpec((1, d), lambda i: (0, 0)), _layer_spec(w_gate, layer, True),
                pl.BlockSpec((None, tm, p.shape[2]), lambda i: (layer, i, 0)), _layer_spec(w_proj, layer, True)]
    args = [h, g.reshape(1, d), w_gate, p, w_proj]
    out_specs = [_row_spec(tm, d)]
    out_shape = [jax.ShapeDtypeStruct((m, d), F32)]
    splits = ()
    if nxt is not None:
        g2, w_in, in_layer, splits, dtypes = nxt
        assert sum(splits) == w_in.shape[2]
        in_specs += [pl.BlockSpec((1, d), lambda i: (0, 0)), _layer_spec(w_in, in_layer, True)]
        args += [g2.reshape(1, d), w_in]
        out_specs += [_row_spec(tm, s) for s in splits]
        out_shape += [jax.ShapeDtypeStruct((m, s), dt) for s, dt in zip(splits, dtypes)]
    return pl.pallas_call(
        functools.partial(_ple_proj_kernel, splits=tuple(splits), nchunk=nchunk),
        grid=(m // tm,),
        in_specs=in_specs,
        out_specs=out_specs,
        out_shape=out_shape,
        compiler_params=_cparams(("parallel",)),
        name="ple_proj",
    )(*args)


def _sb_tri():
    j = np.arange(2 * BLK)[:, None] % BLK
    s = np.arange(2 * BLK)[None, :]
    return ((s >= BLK) | (j >= s)).astype(np.float32)


def _neg_abs(x):
    sign = jnp.uint32(0x80000000)
    return lax.bitcast_convert_type(lax.bitcast_convert_type(x, jnp.uint32) | sign, F32)


def _sb_kernel(q_ref, k_ref, v_ref, tri_ref, o_ref, qs_ref, acc_ref, c_ref):
    n0 = SB_QBLK * pl.program_id(1)
    npair = SB_WIDTH // LANES
    pairs = range(npair)
    cols = [slice(p * LANES, (p + 1) * LANES) for p in pairs]
    lane = lax.broadcasted_iota(jnp.int32, (2 * BLK, BLK), 1)
    row = lax.broadcasted_iota(jnp.int32, (2 * BLK, BLK), 0)
    first_head = row < BLK
    causal = lane < jnp.where(first_head, row, row - BLK)
    own_lanes = first_head == (lane < SB_DIM)
    rn = SB_NEAR_ROWS

    for qb in range(SB_QBLK):
        for p in pairs:
            q2 = q_ref[0, qb * BLK:(qb + 1) * BLK, cols[p]] * (SB_DIM ** -0.5)
            q2 = jnp.concatenate([q2, q2], axis=0)
            qs_ref[qb, p] = jnp.where(own_lanes, q2, jnp.zeros_like(q2))

    def stacked(ref, qb, p, lo, hi):
        if (lo, hi) == (0, BLK):
            return ref[qb, p]
        return jnp.concatenate([ref[qb, p, lo:hi], ref[qb, p, BLK + lo:BLK + hi]], axis=0)

    def run(jobs):
        for job in jobs:
            job["krows"] = pl.ds(pl.multiple_of(job["kb"] * BLK, BLK), job["nblk"] * BLK)
            job["tn"] = [_dot_nt(stacked(qs_ref, job["qb"], p, job["lo"], job["hi"]),
                                 k_ref[0, job["krows"], cols[p]]) * (-LOG2E) for p in pairs]
        for job in jobs:
            job["split"] = {}
            for p in pairs:
                for j in range(job["nblk"]):
                    t = job["tn"][p][:, j * BLK:(j + 1) * BLK]
                    log_keep = jnp.minimum(t, 0.0) - jnp.log2(1.0 + jnp.exp2(_neg_abs(t)))
                    if job["diag"] and j == job["nblk"] - 1:
                        log_keep = jnp.where(causal, log_keep, 0.0)
                    job["split"][p, j] = jnp.concatenate(_split_bf16(log_keep), axis=1)
        for job in jobs:
            job["sums"] = {key: _dot(val, tri_ref[...]) for key, val in job["split"].items()}
        for job in jobs:
            qb, lo, hi, nblk = job["qb"], job["lo"], job["hi"], job["nblk"]
            r = hi - lo
            job["ws"], job["c"] = [], []
            for p in pairs:
                if job["diag"]:
                    c = None
                elif job["after"] is not None:
                    prev = jobs[job["after"]]["c"][p]
                    c = jnp.concatenate([prev[lo:hi], prev[BLK + lo:BLK + hi]], axis=0)
                else:
                    c = stacked(c_ref, qb, p, lo, hi)
                w = [None] * nblk
                for j in range(nblk - 1, -1, -1):
                    s = job["sums"][p, j]
                    incl, total = s[:, :BLK], s[:, BLK:]
                    arg = incl - job["tn"][p][:, j * BLK:(j + 1) * BLK]
                    if c is not None:
                        arg = arg + c
                    wj = jnp.exp2(arg)
                    if job["diag"] and j == nblk - 1:
                        wj = jnp.where(causal, wj, 0.0)
                    w[j] = wj.astype(BF16)
                    c = total if c is None else c + total
                job["ws"].append(w[0] if nblk == 1 else jnp.concatenate(w, axis=1))
                job["c"].append(c)
                if (lo, hi) == (0, BLK):
                    c_ref[qb, p] = c
                else:
                    c_ref[qb, p, lo:hi] = c[:r]
                    c_ref[qb, p, BLK + lo:BLK + hi] = c[r:]
        for job in jobs:
            job["pv"] = [_dot(job["ws"][p], v_ref[0, job["krows"], cols[p]]) for p in pairs]
        for job in jobs:
            qb, lo, hi = job["qb"], job["lo"], job["hi"]
            r = hi - lo
            first_half = lax.broadcasted_iota(jnp.int32, (r, BLK), 1) < SB_DIM
            for p in pairs:
                pv = jnp.where(first_half, job["pv"][p][:r], job["pv"][p][r:])
                if job["diag"]:
                    acc_ref[qb, p] = pv
                else:
                    acc_ref[qb, p, lo:hi] += pv

    def job(qb, kb, nblk, diag, lo=0, hi=BLK, after=None):
        return dict(qb=qb, kb=kb, nblk=nblk, diag=diag, lo=lo, hi=hi, after=after)

    def live(qb, lo, hi):
        c = c_ref[qb]
        return jnp.max(jnp.maximum(c[:, lo:hi], c[:, BLK + lo:BLK + hi])) > SB_DEAD_LOG2

    more = lambda st: (st[0] >= 0) & st[1]

    def rest(qb, kb):
        @pl.when(live(qb, rn, BLK))
        def _():
            run([job(qb, kb, 1, False, rn, BLK)])

        def full_block(state):
            run([job(qb, state[0], 1, False)])
            return state[0] - 1, live(qb, rn, BLK)

        def near_block(state):
            run([job(qb, state[0], 1, False, 0, rn)])
            return state[0] - 1, live(qb, 0, rn)

        kb2, _ = lax.while_loop(more, full_block, (kb - 1, live(qb, rn, BLK)))
        lax.while_loop(more, near_block, (kb2, live(qb, 0, rn)))

    @pl.when(n0 == 0)
    def _():
        run([job(qb, 0, qb + 1, True) for qb in range(SB_QBLK)])

    @pl.when(n0 > 0)
    def _():
        jobs = []
        for qb in range(SB_QBLK):
            jobs.append(job(qb, n0 + qb - 1, 2, True))
            jobs.append(job(qb, n0 + qb - 2, 1, False, 0, rn, after=len(jobs) - 1))
        run(jobs)

        @pl.when(jnp.max(c_ref[...]) > SB_DEAD_LOG2)
        def _():
            for qb in range(SB_QBLK):
                rest(qb, n0 + qb - 2)

    for qb in range(SB_QBLK):
        for p in pairs:
            o_ref[0, qb * BLK:(qb + 1) * BLK, cols[p]] = acc_ref[qb, p].astype(o_ref.dtype)


def stick_breaking(proj):
    b, s, _ = proj.shape
    tri = jnp.asarray(_sb_tri(), BF16)
    npair = SB_WIDTH // LANES
    tq = SB_QBLK * BLK
    assert s % tq == 0
    return pl.pallas_call(
        _sb_kernel,
        grid=(b, s // tq),
        in_specs=[pl.BlockSpec((1, tq, SB_WIDTH), lambda i, n: (i, n, 0)),
                  pl.BlockSpec((1, s, SB_WIDTH), lambda i, n: (i, 0, 1)),
                  pl.BlockSpec((1, s, SB_WIDTH), lambda i, n: (i, 0, 2)),
                  pl.BlockSpec((2 * BLK, 2 * BLK), lambda i, n: (0, 0))],
        out_specs=pl.BlockSpec((1, tq, SB_WIDTH), lambda i, n: (i, n, 0)),
        out_shape=jax.ShapeDtypeStruct((b, s, SB_WIDTH), BF16),
        scratch_shapes=[pltpu.VMEM((SB_QBLK, npair, 2 * BLK, LANES), BF16),
                        pltpu.VMEM((SB_QBLK, npair, BLK, LANES), F32),
                        pltpu.VMEM((SB_QBLK, npair, 2 * BLK, LANES), F32)],
        compiler_params=_cparams(("parallel", "arbitrary")),
        name="stick_breaking",
    )(proj, proj, proj, tri)


def _hg_time_mats():
    t = np.arange(BLK)[:, None]
    j = np.arange(BLK)[None, :]
    same = (t // HG_CHUNK) == (j // HG_CHUNK)
    per_chunk = np.arange(2 * SUBLANES)[:, None] == (j // HG_CHUNK)
    return np.concatenate([same & (j <= t), same, per_chunk], axis=0).astype(np.float32)


def _hgrn_kernel(q_ref, f_ref, i_ref, g_ref, lb_ref, gn_ref, tmat_ref, o_ref,
                 qf_s, kk_s, b_s, iv_s, qt_s, kt_s, dec_s, raw_s, state_s):
    ts = q_ref.shape[1]
    units = [(bi, h) for bi in range(q_ref.shape[0]) for h in range(HG_HEADS)]
    hcols = [slice(h * LANES, (h + 1) * LANES) for h in range(HG_HEADS)]
    cpb = BLK // HG_CHUNK

    @pl.when(pl.program_id(1) == 0)
    def _():
        state_s[...] = jnp.zeros_like(state_s)

    def prep(r, _):
        rows = pl.ds(pl.multiple_of(r * BLK, BLK), BLK)
        for u, (bi, h) in enumerate(units):
            lb = lb_ref[:, hcols[h]]
            fp = f_ref[bi, rows, hcols[h]]
            q = q_ref[bi, rows, hcols[h]]
            e = jnp.exp(-jnp.abs(fp))
            sg = 1.0 / (1.0 + e)
            log_f = jnp.log(lb + (1.0 - lb) * jnp.where(fp >= 0, sg, e * sg))
            kk = (1.0 - lb) * jnp.where(fp >= 0, e * sg, sg)
            qf = q * jax.nn.sigmoid(q)
            hi, lo = _split_bf16(log_f)
            sums = _dot(tmat_ref[...], jnp.concatenate([hi, lo], axis=1))
            sums = sums[:, :LANES] + sums[:, LANES:]
            b = sums[:BLK]
            b_tot = sums[BLK:2 * BLK]
            chunk_tot = sums[2 * BLK:2 * BLK + cpb]
            qf_s[u, rows, :] = qf
            kk_s[u, rows, :] = kk
            b_s[u, rows, :] = b
            iv_s[u, rows, :] = i_ref[bi, rows, hcols[h]]
            qt_s[u, rows, :] = (qf * jnp.exp(b)).astype(BF16)
            kt_s[u, rows, :] = (kk * jnp.exp(b_tot - b)).astype(BF16)
            dec_s[u, pl.ds(pl.multiple_of(r * cpb, cpb), cpb), :] = jnp.exp(chunk_tot)
        return 0

    lax.fori_loop(0, ts // BLK, prep, 0)

    rowi = lax.broadcasted_iota(jnp.int32, (SUBLANES, 1), 0)

    def chunk(c, _):
        r0 = pl.multiple_of(c * HG_CHUNK, HG_CHUNK)
        rows = pl.ds(r0, HG_CHUNK)
        inter = [_dot_nt(qt_s[u, rows, :], state_s[u].astype(BF16)) for u in range(len(units))]
        for u in range(len(units)):
            for g in range(HG_CHUNK // SUBLANES):
                g0 = g * SUBLANES
                grows = pl.ds(r0 + g0, SUBLANES)
                qf, b = qf_s[u, grows, :], b_s[u, grows, :]
                o = jnp.zeros((SUBLANES, LANES), F32)
                for s in range(g0 + SUBLANES):
                    one = pl.ds(r0 + s, 1)
                    b_key, k_key, v_key = b_s[u, one, :], kk_s[u, one, :], iv_s[u, one, :]
                    score = jnp.sum(qf * k_key * jnp.exp(b - b_key), axis=-1, keepdims=True)
                    if s > g0:
                        score = jnp.where(rowi >= s - g0, score, 0.0)
                    o = o + score * v_key
                raw_s[u, grows, :] = o + inter[u][g0:g0 + SUBLANES]
        upd = [_dot_tn(iv_s[u, rows, :].astype(BF16), kt_s[u, rows, :]) for u in range(len(units))]
        for u in range(len(units)):
            state_s[u] = state_s[u] * dec_s[u, pl.ds(c, 1), :] + upd[u]
        return 0

    lax.fori_loop(0, ts // HG_CHUNK, chunk, 0, unroll=2)

    def finish(r, _):
        rows = pl.ds(pl.multiple_of(r * BLK, BLK), BLK)
        for u, (bi, h) in enumerate(units):
            g = g_ref[bi, rows, hcols[h]]
            y = _rms(raw_s[u, rows, :], gn_ref[...]) * (g * jax.nn.sigmoid(g))
            o_ref[bi, rows, hcols[h]] = y.astype(o_ref.dtype)
        return 0

    lax.fori_loop(0, ts // BLK, finish, 0)


def hgrn2(proj, lb, out_norm, ts=512):
    b, s, _ = proj.shape
    assert s % ts == 0 and ts % BLK == 0 and b % HG_BATCH == 0
    tmat = jnp.asarray(_hg_time_mats(), BF16)
    seq_blk = lambda part: pl.BlockSpec((HG_BATCH, ts, HG_QK), lambda i, t: (i, t, part))
    const = lambda i, t: (0, 0)
    nu = HG_BATCH * HG_HEADS
    return pl.pallas_call(
        _hgrn_kernel,
        grid=(b // HG_BATCH, s // ts),
        in_specs=[seq_blk(0), seq_blk(1), seq_blk(2), seq_blk(3),
                  pl.BlockSpec((1, HG_QK), const),
                  pl.BlockSpec((1, HG_DV), const),
                  pl.BlockSpec(tmat.shape, const)],
        out_specs=pl.BlockSpec((HG_BATCH, ts, HG_V), lambda i, t: (i, t, 0)),
        out_shape=jax.ShapeDtypeStruct((b, s, HG_V), BF16),
        scratch_shapes=[pltpu.VMEM((nu, ts, HG_DK), F32), pltpu.VMEM((nu, ts, HG_DK), F32),
                        pltpu.VMEM((nu, ts, HG_DK), F32), pltpu.VMEM((nu, ts, HG_DV), F32),
                        pltpu.VMEM((nu, ts, HG_DK), BF16), pltpu.VMEM((nu, ts, HG_DK), BF16),
                        pltpu.VMEM((nu, ts // HG_CHUNK, HG_DK), F32),
                        pltpu.VMEM((nu, ts, HG_DV), F32),
                        pltpu.VMEM((nu, HG_DV, HG_DK), F32)],
        compiler_params=_cparams(("parallel", "arbitrary")),
        name="hgrn2",
    )(proj, proj, proj, proj, lb.reshape(1, HG_QK), out_norm.reshape(1, HG_DV), tmat)


def _t5_band_buckets():
    t = np.arange(WINDOW)[:, None]
    s = np.arange(2 * WINDOW)[None, :]
    dist = t + WINDOW - s
    max_exact = N_BUCKETS // 2
    large = max_exact + (np.log(np.maximum(dist, max_exact) / max_exact) / math.log(MAX_DISTANCE / max_exact)
                         * (N_BUCKETS - max_exact)).astype(np.int32)
    large = np.minimum(large, N_BUCKETS - 1)
    band = (dist >= 0) & (dist < WINDOW)
    return np.where(dist < max_exact, np.maximum(dist, 0), large).astype(np.int32), band


def _swa_kernel(sink_ref, q_ref, kc_ref, kp_ref, vc_ref, vp_ref, bias_ref, qg_ref, kg_ref, gmat_ref, o_ref):
    n = pl.program_id(1)
    w2 = 2 * WINDOW
    no_prev = jnp.where(n > 0, 0.0, -jnp.inf)
    q_groups = SW_HEADS * SW_DIM // LANES
    kv_groups = SW_KV_HEADS * SW_DIM // LANES
    per_kv = q_groups // kv_groups
    kv_rows = (SW_QBLK + 1) * WINDOW
    grp = lambda ref, p: ref[0, :, p * LANES:(p + 1) * LANES]

    def head_norm(x, gain):
        sq = jnp.concatenate(_split_bf16(x * x), axis=1)
        ssq = _dot(sq, gmat_ref[...])
        return x * lax.rsqrt(ssq * (1.0 / SW_DIM) + EPS) * gain

    qn = head_norm(jnp.concatenate([grp(q_ref, p)[i * WINDOW:(i + 1) * WINDOW]
                                    for i in range(SW_QBLK) for p in range(q_groups)], axis=0), qg_ref[...])
    kn = head_norm(jnp.concatenate([x for r in range(kv_groups) for x in (grp(kp_ref, r), grp(kc_ref, r))], axis=0),
                   kg_ref[...])
    v = jnp.concatenate([x for r in range(kv_groups) for x in (grp(vp_ref, r), grp(vc_ref, r))], axis=0)
    kn_rot, v_rot = pltpu.roll(kn, SW_DIM, 1), pltpu.roll(v, SW_DIM, 1)
    kn, kn_rot, v, v_rot = (a.astype(BF16) for a in (kn, kn_rot, v, v_rot))
    ones = jnp.ones((w2, LANES), BF16)

    rows_q = per_kv * WINDOW
    lane = lax.broadcasted_iota(jnp.int32, (rows_q, LANES), 1)
    row = lax.broadcasted_iota(jnp.int32, (rows_q, LANES), 0)
    same_half = (lane < SW_DIM) == (row < rows_q // 2)
    blocks = [(i, r) for i in range(SW_QBLK) for r in range(kv_groups)]
    keys = lambda i, r: slice(r * kv_rows + i * WINDOW, r * kv_rows + i * WINDOW + w2)

    logits = {}
    for i, r in blocks:
        qr = qn[(i * kv_groups + r) * rows_q:(i * kv_groups + r + 1) * rows_q]
        zero = jnp.zeros_like(qr)
        logits[i, r] = (_dot_nt(jnp.where(same_half, qr, zero).astype(BF16), kn[keys(i, r)]),
                        _dot_nt(jnp.where(same_half, zero, qr).astype(BF16), kn_rot[keys(i, r)]))

    weights, sink_terms = {}, {}
    for i, r in blocks:
        w_r, s_r = ([], []), ([], [])
        for j in range(per_kv):
            p = r * per_kv + j
            second = j >= per_kv // 2
            for variant, h in ((0, 2 * p + int(second)), (1, 2 * p + int(not second))):
                lg = logits[i, r][variant][j * WINDOW:(j + 1) * WINDOW]
                bias_prev = bias_ref[h, :, :WINDOW]
                lp = lg[:, :WINDOW] + (bias_prev + no_prev if i == 0 else bias_prev)
                lc = lg[:, WINDOW:] + bias_ref[h, :, WINDOW:]
                sink = sink_ref[h]
                m = jnp.maximum(jnp.max(jnp.maximum(lp, lc), axis=-1, keepdims=True), sink)
                w_r[variant].append(jnp.concatenate([jnp.exp(lp - m), jnp.exp(lc - m)], axis=1).astype(BF16))
                s_r[variant].append(jnp.broadcast_to(jnp.exp(sink - m), (WINDOW, LANES)))
        weights[i, r] = tuple(jnp.concatenate(x, axis=0) for x in w_r)
        sink_terms[i, r] = tuple(jnp.concatenate(x, axis=0) for x in s_r)

    outs = {(i, r): (_dot(weights[i, r][0], jnp.concatenate([v[keys(i, r)], ones], axis=1)),
                     _dot(weights[i, r][1], jnp.concatenate([v_rot[keys(i, r)], ones], axis=1)))
            for i, r in blocks}
    for i, r in blocks:
        o = [outs[i, r][t][:, :LANES] / (outs[i, r][t][:, LANES:] + sink_terms[i, r][t]) for t in range(2)]
        o = jnp.where(same_half, o[0], o[1]).astype(o_ref.dtype)
        for j in range(per_kv):
            p = r * per_kv + j
            o_ref[0, i * WINDOW:(i + 1) * WINDOW, p * LANES:(p + 1) * LANES] = o[j * WINDOW:(j + 1) * WINDOW]


def _swa_group_sum():
    i = np.arange(2 * LANES)[:, None] % LANES
    j = np.arange(LANES)[None, :]
    return (i // SW_DIM == j // SW_DIM).astype(np.float32)


def sliding_window(proj, q_norm, k_norm, sinks, rel_bias):
    b, s, _ = proj.shape
    bucket, band = _t5_band_buckets()
    onehot = jnp.asarray(bucket[..., None] == np.arange(N_BUCKETS), F32)
    bias = jnp.einsum('tsb,bh->hts', onehot, rel_bias.astype(F32), precision=lax.Precision.HIGHEST)
    bias = jnp.where(band, bias, -jnp.inf)
    kv_w = SW_KV_HEADS * SW_DIM
    q_w = SW_HEADS * SW_DIM
    k_blk = q_w // kv_w
    per_lane_group = LANES // SW_DIM
    q_gain = jnp.tile(q_norm.astype(F32), per_lane_group).reshape(1, LANES) * (SW_DIM ** -0.5)
    k_gain = jnp.tile(k_norm.astype(F32), per_lane_group).reshape(1, LANES)
    tq = SW_QBLK * WINDOW
    assert s % tq == 0
    prev = lambda n: jnp.maximum(SW_QBLK * n - 1, 0)
    const2 = lambda i, n: (0, 0)
    return pl.pallas_call(
        _swa_kernel,
        grid=(b, s // tq),
        in_specs=[pl.BlockSpec(memory_space=pltpu.SMEM),
                  pl.BlockSpec((1, tq, q_w), lambda i, n: (i, n, 0)),
                  pl.BlockSpec((1, tq, kv_w), lambda i, n: (i, n, k_blk)),
                  pl.BlockSpec((1, WINDOW, kv_w), lambda i, n: (i, prev(n), k_blk)),
                  pl.BlockSpec((1, tq, kv_w), lambda i, n: (i, n, k_blk + 1)),
                  pl.BlockSpec((1, WINDOW, kv_w), lambda i, n: (i, prev(n), k_blk + 1)),
                  pl.BlockSpec((SW_HEADS, WINDOW, 2 * WINDOW), lambda i, n: (0, 0, 0)),
                  pl.BlockSpec((1, LANES), const2),
                  pl.BlockSpec((1, LANES), const2),
                  pl.BlockSpec((2 * LANES, LANES), const2)],
        out_specs=pl.BlockSpec((1, tq, q_w), lambda i, n: (i, n, 0)),
        out_shape=jax.ShapeDtypeStruct((b, s, q_w), BF16),
        compiler_params=_cparams(("parallel", "parallel")),
        name="sliding_window",
    )(sinks.astype(F32), proj, proj, proj, proj, proj, bias, q_gain, k_gain,
      jnp.asarray(_swa_group_sum(), BF16))


def kernel(x, p, mix_norm, ab_w_in, hg_lb_logits, hg_out_norm, ab_w_out, c_w_in, q_norm, k_norm, sinks,
           rel_bias, c_w_out, ffn_norm, ffn_up, ffn_conv, ffn_conv_b, ffn_down, ple_norm, ple_gate, ple_proj):
    b, s, d = x.shape
    depth = p.shape[0]
    m = b * s
    lb_cum = jnp.cumsum(jax.nn.softmax(hg_lb_logits.astype(F32), axis=0), axis=0)
    lower_bounds = lb_cum - lb_cum[0]
    ab_w_in, ab_w_out, c_w_in, c_w_out, ffn_up, ffn_down, ple_gate, ple_proj = (
        w.astype(BF16) for w in (ab_w_in, ab_w_out, c_w_in, c_w_out, ffn_up, ffn_down, ple_gate, ple_proj))
    conv_b = ffn_conv_b.reshape(depth, 1, -1)
    p = p.reshape(depth, m, -1)

    def in_proj(i):
        if i % 2 == 0:
            return mix_norm[i], ab_w_in, i // 2, (3 * SB_WIDTH, 2 * HG_QK + 2 * HG_V), (BF16, F32)
        return mix_norm[i], c_w_in, i // 2, (c_w_in.shape[2],), (F32,)

    h = x.reshape(m, d)
    proj = norm_proj(h, *in_proj(0))
    for i in range(depth):
        j = i // 2
        if i % 2 == 0:
            att, gates = proj
            o_a = stick_breaking(att.reshape(b, s, -1))
            o_b = hgrn2(gates.reshape(b, s, -1), lower_bounds[j], hg_out_norm[j])
            parts, w_out = [o_a.reshape(m, -1), o_b.reshape(m, -1)], ab_w_out
        else:
            (qkv,) = proj
            o = sliding_window(qkv.reshape(b, s, -1), q_norm[j], k_norm[j], sinks[j], rel_bias)
            parts, w_out = [o.reshape(m, -1)], c_w_out
        h = mix_ffn(parts, w_out, j, h, ffn_norm[i], ffn_up, ffn_conv, conv_b, ffn_down, i, s)
        h, *proj = ple_and_proj(h, ple_norm[i], ple_gate, p, ple_proj, i, in_proj(i + 1) if i + 1 < depth else None)
    return h.reshape(b, s, d)
```

```python
import functools
import math

import numpy as np
import jax
import jax.numpy as jnp
from jax import lax
from jax.experimental import pallas as pl
from jax.experimental.pallas import tpu as pltpu

F32 = jnp.float32
BF16 = jnp.bfloat16

D_MODEL = 1024
PLE_DIM = 256
EPS = 1e-6
SB_HEADS = 8
SB_DIM = 64
SB_WIDTH = SB_HEADS * SB_DIM
HG_HEADS = 4
HG_DK = 128
HG_DV = 128
HG_QK = HG_HEADS * HG_DK
HG_V = HG_HEADS * HG_DV
SW_HEADS = 16
SW_KV_HEADS = 4
SW_DIM = 64
SW_GROUP = SW_HEADS // SW_KV_HEADS
WINDOW = 128
SW_QBLK = 2
N_BUCKETS = 32
MAX_DISTANCE = 128
D_FF = 2816
CONV_W = 3

LANES = 128
SUBLANES = 8
BLK = 128
HG_CHUNK = 16
HG_BATCH = 2
FF_CHUNK = 256
SB_DEAD_LOG2 = -150.0
SB_NEAR_ROWS = 32
SB_QBLK = 2
LOG2E = 1.4426950408889634
VMEM_LIMIT = 56 * 1024 * 1024


def _cparams(sem):
    return pltpu.CompilerParams(dimension_semantics=sem, vmem_limit_bytes=VMEM_LIMIT)


def _rms(x, g):
    ms = jnp.mean(x * x, axis=-1, keepdims=True)
    return x * lax.rsqrt(ms + EPS) * g


def _dot(a, b):
    return jnp.dot(a, b, preferred_element_type=F32)


def _dot_nt(a, b):
    return lax.dot_general(a, b, (((1,), (1,)), ((), ())), preferred_element_type=F32)


def _dot_tn(a, b):
    return lax.dot_general(a, b, (((0,), (0,)), ((), ())), preferred_element_type=F32)


def _split_bf16(x):
    hi = x.astype(BF16)
    lo = (x - hi.astype(F32)).astype(BF16)
    return hi, lo


def _layer_spec(stack, layer, single_buffer=False):
    zeros = (0,) * (stack.ndim - 1)
    kwargs = dict(pipeline_mode=pl.Buffered(1)) if single_buffer else {}
    return pl.BlockSpec((None,) + stack.shape[1:], lambda *_: (layer,) + zeros, **kwargs)


def _row_spec(tm, width):
    return pl.BlockSpec((tm, width), lambda i: (i, 0))


def _project(hn, w_ref, o_refs, splits, nchunk):
    col = 0
    for o_ref, width in zip(o_refs, splits):
        for c0 in range(0, width, nchunk):
            cw = min(nchunk, width - c0)
            o_ref[:, c0:c0 + cw] = _dot(hn, w_ref[:, col + c0:col + c0 + cw]).astype(o_ref.dtype)
        col += width


def _norm_proj_kernel(h_ref, g_ref, w_ref, *o_refs, splits, nchunk):
    _project(_rms(h_ref[...], g_ref[...]).astype(BF16), w_ref, o_refs, splits, nchunk)


def norm_proj(h, g, w_stack, layer, splits, dtypes, tm=512, nchunk=512):
    m, k = h.shape
    assert sum(splits) == w_stack.shape[2] and m % tm == 0
    return pl.pallas_call(
        functools.partial(_norm_proj_kernel, splits=tuple(splits), nchunk=nchunk),
        grid=(m // tm,),
        in_specs=[_row_spec(tm, k), pl.BlockSpec((1, k), lambda i: (0, 0)), _layer_spec(w_stack, layer)],
        out_specs=[_row_spec(tm, s) for s in splits],
        out_shape=[jax.ShapeDtypeStruct((m, s), d) for s, d in zip(splits, dtypes)],
        compiler_params=_cparams(("parallel",)),
        name="norm_proj",
    )(h, g.reshape(1, k), w_stack)


def _mix_ffn_kernel(*refs, n_parts, tiles_per_seq):
    a_refs = refs[:n_parts]
    wo_ref, h_ref, g_ref, wup_ref, cw_ref, cb_ref, wdn_ref, o_ref, hn_ref, act_ref, halo_ref = refs[n_parts:]
    tm = h_ref.shape[0]
    x = h_ref[...]
    k0 = 0
    for a_ref in a_refs:
        k = a_ref.shape[1]
        x = x + _dot(a_ref[...], wo_ref[k0:k0 + k, :])
        k0 += k
    hn_ref[...] = _rms(x, g_ref[...]).astype(BF16)
    seq_start = (pl.program_id(0) % tiles_per_seq) == 0
    row = lax.broadcasted_iota(jnp.int32, (tm, FF_CHUNK), 0)

    def conv_cols(c0):
        cols = slice(c0, c0 + FF_CHUNK)
        u = _dot(hn_ref[...], wup_ref[:, cols])
        prev = jnp.where(seq_start, 0.0, halo_ref[:, cols])
        halo_ref[:, cols] = u[tm - SUBLANES:, :]
        p1 = prev[SUBLANES - 1:SUBLANES, :]
        p2 = prev[SUBLANES - 2:SUBLANES - 1, :]
        u1 = jnp.where(row == 0, p1, pltpu.roll(u, 1, 0))
        u2 = jnp.where(row == 0, p2, jnp.where(row == 1, p1, pltpu.roll(u, 2, 0)))
        cw = cw_ref[:, cols]
        return cw[0:1, :] * u2 + cw[1:2, :] * u1 + cw[2:3, :] * u + cb_ref[:, cols]

    for c in range(D_FF // FF_CHUNK):
        gate = conv_cols(c * FF_CHUNK)
        up = conv_cols(D_FF + c * FF_CHUNK)
        act_ref[:, c * FF_CHUNK:(c + 1) * FF_CHUNK] = (gate * jax.nn.sigmoid(gate) * up).astype(BF16)
    o_ref[...] = x + _dot(act_ref[...], wdn_ref[...])


def mix_ffn(parts, w_out, out_layer, h, g, w_up, conv_w, conv_b, w_down, layer, seq, tm=1024):
    m, d = h.shape
    f2 = w_up.shape[2]
    assert seq % tm == 0 and D_FF % FF_CHUNK == 0 and sum(a.shape[1] for a in parts) == w_out.shape[1]
    return pl.pallas_call(
        functools.partial(_mix_ffn_kernel, n_parts=len(parts), tiles_per_seq=seq // tm),
        grid=(m // tm,),
        in_specs=[_row_spec(tm, a.shape[1]) for a in parts]
                 + [_layer_spec(w_out, out_layer, True), _row_spec(tm, d), pl.BlockSpec((1, d), lambda i: (0, 0)),
                    _layer_spec(w_up, layer, True), _layer_spec(conv_w, layer), _layer_spec(conv_b, layer),
                    _layer_spec(w_down, layer, True)],
        out_specs=_row_spec(tm, d),
        out_shape=jax.ShapeDtypeStruct((m, d), F32),
        scratch_shapes=[pltpu.VMEM((tm, d), BF16),
                        pltpu.VMEM((tm, D_FF), BF16),
                        pltpu.VMEM((SUBLANES, f2), F32)],
        compiler_params=_cparams(("arbitrary",)),
        name="mix_ffn",
    )(*parts, w_out, h, g.reshape(1, d), w_up, conv_w, conv_b, w_down)


def _ple_proj_kernel(*refs, splits, nchunk):
    h_ref, g_ref, wg_ref, p_ref, wp_ref = refs[:5]
    x = h_ref[...]
    gate = jax.nn.sigmoid(_dot(_rms(x, g_ref[...]).astype(BF16), wg_ref[...]))
    x = x + gate * _dot(p_ref[...].astype(BF16), wp_ref[...])
    if splits:
        g2_ref, win_ref, o_ref, *proj_refs = refs[5:]
        _project(_rms(x, g2_ref[...]).astype(BF16), win_ref, proj_refs, splits, nchunk)
    else:
        (o_ref,) = refs[5:]
    o_ref[...] = x


def ple_and_proj(h, g, w_gate, p, w_proj, layer, nxt=None, tm=512, nchunk=512):
    m, d = h.shape
    in_specs = [_row_spec(tm, d), pl.BlockSpec((1, d), lambda i: (0, 0)), _layer_spec(w_gate, layer, True),
                pl.BlockSpec((None, tm, p.shape[2]), lambda i: (layer, i, 0)), _layer_spec(w_proj, layer, True)]
    args = [h, g.reshape(1, d), w_gate, p, w_proj]
    out_specs = [_row_spec(tm, d)]
    out_shape = [jax.ShapeDtypeStruct((m, d), F32)]
    splits = ()
    if nxt is not None:
        g2, w_in, in_layer, splits, dtypes = nxt
        assert sum(splits) == w_in.shape[2]
        in_specs += [pl.BlockSpec((1, d), lambda i: (0, 0)), _layer_spec(w_in, in_layer, True)]
        args += [g2.reshape(1, d), w_in]
        out_specs += [_row_spec(tm, s) for s in splits]
        out_shape += [jax.ShapeDtypeStruct((m, s), dt) for s, dt in zip(splits, dtypes)]
    return pl.pallas_call(
        functools.partial(_ple_proj_kernel, splits=tuple(splits), nchunk=nchunk),
        grid=(m // tm,),
        in_specs=in_specs,
        out_specs=out_specs,
        out_shape=out_shape,
        compiler_params=_cparams(("parallel",)),
        name="ple_proj",
    )(*args)


def _sb_tri():
    j = np.arange(2 * BLK)[:, None] % BLK
    s = np.arange(2 * BLK)[None, :]
    return ((s >= BLK) | (j >= s)).astype(np.float32)


def _neg_abs(x):
    sign = jnp.uint32(0x80000000)
    return lax.bitcast_convert_type(lax.bitcast_convert_type(x, jnp.uint32) | sign, F32)


def _sb_kernel(q_ref, k_ref, v_ref, tri_ref, o_ref, qs_ref, acc_ref, c_ref):
    n0 = SB_QBLK * pl.program_id(1)
    npair = SB_WIDTH // LANES
    pairs = range(npair)
    cols = [slice(p * LANES, (p + 1) * LANES) for p in pairs]
    lane = lax.broadcasted_iota(jnp.int32, (2 * BLK, BLK), 1)
    row = lax.broadcasted_iota(jnp.int32, (2 * BLK, BLK), 0)
    first_head = row < BLK
    causal = lane < jnp.where(first_head, row, row - BLK)
    own_lanes = first_head == (lane < SB_DIM)
    rn = SB_NEAR_ROWS

    for qb in range(SB_QBLK):
        for p in pairs:
            q2 = q_ref[0, qb * BLK:(qb + 1) * BLK, cols[p]] * (SB_DIM ** -0.5)
            q2 = jnp.concatenate([q2, q2], axis=0)
            qs_ref[qb, p] = jnp.where(own_lanes, q2, jnp.zeros_like(q2))

    def stacked(ref, qb, p, lo, hi):
        if (lo, hi) == (0, BLK):
            return ref[qb, p]
        return jnp.concatenate([ref[qb, p, lo:hi], ref[qb, p, BLK + lo:BLK + hi]], axis=0)

    def run(jobs):
        for job in jobs:
            job["krows"] = pl.ds(pl.multiple_of(job["kb"] * BLK, BLK), job["nblk"] * BLK)
            job["tn"] = [_dot_nt(stacked(qs_ref, job["qb"], p, job["lo"], job["hi"]),
                                 k_ref[0, job["krows"], cols[p]]) * (-LOG2E) for p in pairs]
        for job in jobs:
            job["split"] = {}
            for p in pairs:
                for j in range(job["nblk"]):
                    t = job["tn"][p][:, j * BLK:(j + 1) * BLK]
                    log_keep = jnp.minimum(t, 0.0) - jnp.log2(1.0 + jnp.exp2(_neg_abs(t)))
                    if job["diag"] and j == job["nblk"] - 1:
                        log_keep = jnp.where(causal, log_keep, 0.0)
                    job["split"][p, j] = jnp.concatenate(_split_bf16(log_keep), axis=1)
        for job in jobs:
            job["sums"] = {key: _dot(val, tri_ref[...]) for key, val in job["split"].items()}
        for job in jobs:
            qb, lo, hi, nblk = job["qb"], job["lo"], job["hi"], job["nblk"]
            r = hi - lo
            job["ws"], job["c"] = [], []
            for p in pairs:
                if job["diag"]:
                    c = None
                elif job["after"] is not None:
                    prev = jobs[job["after"]]["c"][p]
                    c = jnp.concatenate([prev[lo:hi], prev[BLK + lo:BLK + hi]], axis=0)
                else:
                    c = stacked(c_ref, qb, p, lo, hi)
                w = [None] * nblk
                for j in range(nblk - 1, -1, -1):
                    s = job["sums"][p, j]
                    incl, total = s[:, :BLK], s[:, BLK:]
                    arg = incl - job["tn"][p][:, j * BLK:(j + 1) * BLK]
                    if c is not None:
                        arg = arg + c
                    wj = jnp.exp2(arg)
                    if job["diag"] and j == nblk - 1:
                        wj = jnp.where(causal, wj, 0.0)
                    w[j] = wj.astype(BF16)
                    c = total if c is None else c + total
                job["ws"].append(w[0] if nblk == 1 else jnp.concatenate(w, axis=1))
                job["c"].append(c)
                if (lo, hi) == (0, BLK):
                    c_ref[qb, p] = c
                else:
                    c_ref[qb, p, lo:hi] = c[:r]
                    c_ref[qb, p, BLK + lo:BLK + hi] = c[r:]
        for job in jobs:
            job["pv"] = [_dot(job["ws"][p], v_ref[0, job["krows"], cols[p]]) for p in pairs]
        for job in jobs:
            qb, lo, hi = job["qb"], job["lo"], job["hi"]
            r = hi - lo
            first_half = lax.broadcasted_iota(jnp.int32, (r, BLK), 1) < SB_DIM
            for p in pairs:
                pv = jnp.where(first_half, job["pv"][p][:r], job["pv"][p][r:])
                if job["diag"]:
                    acc_ref[qb, p] = pv
                else:
                    acc_ref[qb, p, lo:hi] += pv

    def job(qb, kb, nblk, diag, lo=0, hi=BLK, after=None):
        return dict(qb=qb, kb=kb, nblk=nblk, diag=diag, lo=lo, hi=hi, after=after)

    def live(qb, lo, hi):
        c = c_ref[qb]
        return jnp.max(jnp.maximum(c[:, lo:hi], c[:, BLK + lo:BLK + hi])) > SB_DEAD_LOG2

    more = lambda st: (st[0] >= 0) & st[1]

    def rest(qb, kb):
        @pl.when(live(qb, rn, BLK))
        def _():
            run([job(qb, kb, 1, False, rn, BLK)])

        def full_block(state):
            run([job(qb, state[0], 1, False)])
            return state[0] - 1, live(qb, rn, BLK)

        def near_block(state):
            run([job(qb, state[0], 1, False, 0, rn)])
            return state[0] - 1, live(qb, 0, rn)

        kb2, _ = lax.while_loop(more, full_block, (kb - 1, live(qb, rn, BLK)))
        lax.while_loop(more, near_block, (kb2, live(qb, 0, rn)))

    @pl.when(n0 == 0)
    def _():
        run([job(qb, 0, qb + 1, True) for qb in range(SB_QBLK)])

    @pl.when(n0 > 0)
    def _():
        jobs = []
        for qb in range(SB_QBLK):
            jobs.append(job(qb, n0 + qb - 1, 2, True))
            jobs.append(job(qb, n0 + qb - 2, 1, False, 0, rn, after=len(jobs) - 1))
        run(jobs)

        @pl.when(jnp.max(c_ref[...]) > SB_DEAD_LOG2)
        def _():
            for qb in range(SB_QBLK):
                rest(qb, n0 + qb - 2)

    for qb in range(SB_QBLK):
        for p in pairs:
            o_ref[0, qb * BLK:(qb + 1) * BLK, cols[p]] = acc_ref[qb, p].astype(o_ref.dtype)


def stick_breaking(proj):
    b, s, _ = proj.shape
    tri = jnp.asarray(_sb_tri(), BF16)
    npair = SB_WIDTH // LANES
    tq = SB_QBLK * BLK
    assert s % tq == 0 and SB_QBLK == 2
    return pl.pallas_call(
        _sb_kernel,
        grid=(b, s // tq),
        in_specs=[pl.BlockSpec((1, tq, SB_WIDTH), lambda i, n: (i, n, 0)),
                  pl.BlockSpec((1, s, SB_WIDTH), lambda i, n: (i, 0, 1)),
                  pl.BlockSpec((1, s, SB_WIDTH), lambda i, n: (i, 0, 2)),
                  pl.BlockSpec((2 * BLK, 2 * BLK), lambda i, n: (0, 0))],
        out_specs=pl.BlockSpec((1, tq, SB_WIDTH), lambda i, n: (i, n, 0)),
        out_shape=jax.ShapeDtypeStruct((b, s, SB_WIDTH), BF16),
        scratch_shapes=[pltpu.VMEM((SB_QBLK, npair, 2 * BLK, LANES), BF16),
                        pltpu.VMEM((SB_QBLK, npair, BLK, LANES), F32),
                        pltpu.VMEM((SB_QBLK, npair, 2 * BLK, LANES), F32)],
        compiler_params=_cparams(("parallel", "arbitrary")),
        name="stick_breaking",
    )(proj, proj, proj, tri)


def _hg_time_mats():
    t = np.arange(BLK)[:, None]
    j = np.arange(BLK)[None, :]
    same = (t // HG_CHUNK) == (j // HG_CHUNK)
    per_chunk = np.arange(2 * SUBLANES)[:, None] == (j // HG_CHUNK)
    return np.concatenate([same & (j <= t), same, per_chunk], axis=0).astype(np.float32)


def _hgrn_kernel(q_ref, f_ref, i_ref, g_ref, lb_ref, gn_ref, tmat_ref, o_ref,
                 qf_s, kk_s, b_s, iv_s, qt_s, kt_s, dec_s, raw_s, state_s):
    ts = q_ref.shape[1]
    units = [(bi, h) for bi in range(q_ref.shape[0]) for h in range(HG_HEADS)]
    hcols = [slice(h * LANES, (h + 1) * LANES) for h in range(HG_HEADS)]
    cpb = BLK // HG_CHUNK

    @pl.when(pl.program_id(1) == 0)
    def _():
        state_s[...] = jnp.zeros_like(state_s)

    def prep(r, _):
        rows = pl.ds(pl.multiple_of(r * BLK, BLK), BLK)
        for u, (bi, h) in enumerate(units):
            lb = lb_ref[:, hcols[h]]
            fp = f_ref[bi, rows, hcols[h]]
            q = q_ref[bi, rows, hcols[h]]
            e = jnp.exp(-jnp.abs(fp))
            sg = 1.0 / (1.0 + e)
            log_f = jnp.log(lb + (1.0 - lb) * jnp.where(fp >= 0, sg, e * sg))
            kk = (1.0 - lb) * jnp.where(fp >= 0, e * sg, sg)
            qf = q * jax.nn.sigmoid(q)
            hi, lo = _split_bf16(log_f)
            sums = _dot(tmat_ref[...], jnp.concatenate([hi, lo], axis=1))
            sums = sums[:, :LANES] + sums[:, LANES:]
            b = sums[:BLK]
            b_tot = sums[BLK:2 * BLK]
            chunk_tot = sums[2 * BLK:2 * BLK + cpb]
            qf_s[u, rows, :] = qf
            kk_s[u, rows, :] = kk
            b_s[u, rows, :] = b
            iv_s[u, rows, :] = i_ref[bi, rows, hcols[h]]
            qt_s[u, rows, :] = (qf * jnp.exp(b)).astype(BF16)
            kt_s[u, rows, :] = (kk * jnp.exp(b_tot - b)).astype(BF16)
            dec_s[u, pl.ds(pl.multiple_of(r * cpb, cpb), cpb), :] = jnp.exp(chunk_tot)
        return 0

    lax.fori_loop(0, ts // BLK, prep, 0)

    rowi = lax.broadcasted_iota(jnp.int32, (SUBLANES, 1), 0)

    def chunk(c, _):
        r0 = pl.multiple_of(c * HG_CHUNK, HG_CHUNK)
        rows = pl.ds(r0, HG_CHUNK)
        inter = [_dot_nt(qt_s[u, rows, :], state_s[u].astype(BF16)) for u in range(len(units))]
        for u in range(len(units)):
            for g in range(HG_CHUNK // SUBLANES):
                g0 = g * SUBLANES
                grows = pl.ds(r0 + g0, SUBLANES)
                qf, b = qf_s[u, grows, :], b_s[u, grows, :]
                o = jnp.zeros((SUBLANES, LANES), F32)
                for s in range(g0 + SUBLANES):
                    one = pl.ds(r0 + s, 1)
                    b_key, k_key, v_key = b_s[u, one, :], kk_s[u, one, :], iv_s[u, one, :]
                    score = jnp.sum(qf * k_key * jnp.exp(b - b_key), axis=-1, keepdims=True)
                    if s > g0:
                        score = jnp.where(rowi >= s - g0, score, 0.0)
                    o = o + score * v_key
                raw_s[u, grows, :] = o + inter[u][g0:g0 + SUBLANES]
        upd = [_dot_tn(iv_s[u, rows, :].astype(BF16), kt_s[u, rows, :]) for u in range(len(units))]
        for u in range(len(units)):
            state_s[u] = state_s[u] * dec_s[u, pl.ds(c, 1), :] + upd[u]
        return 0

    lax.fori_loop(0, ts // HG_CHUNK, chunk, 0, unroll=2)

    def finish(r, _):
        rows = pl.ds(pl.multiple_of(r * BLK, BLK), BLK)
        for u, (bi, h) in enumerate(units):
            g = g_ref[bi, rows, hcols[h]]
            y = _rms(raw_s[u, rows, :], gn_ref[...]) * (g * jax.nn.sigmoid(g))
            o_ref[bi, rows, hcols[h]] = y.astype(o_ref.dtype)
        return 0

    lax.fori_loop(0, ts // BLK, finish, 0)


def hgrn2(proj, lb, out_norm, ts=512):
    b, s, _ = proj.shape
    assert s % ts == 0 and ts % BLK == 0 and b % HG_BATCH == 0
    tmat = jnp.asarray(_hg_time_mats(), BF16)
    seq_blk = lambda part: pl.BlockSpec((HG_BATCH, ts, HG_QK), lambda i, t: (i, t, part))
    const = lambda i, t: (0, 0)
    nu = HG_BATCH * HG_HEADS
    return pl.pallas_call(
        _hgrn_kernel,
        grid=(b // HG_BATCH, s // ts),
        in_specs=[seq_blk(0), seq_blk(1), seq_blk(2), seq_blk(3),
                  pl.BlockSpec((1, HG_QK), const),
                  pl.BlockSpec((1, HG_DV), const),
                  pl.BlockSpec(tmat.shape, const)],
        out_specs=pl.BlockSpec((HG_BATCH, ts, HG_V), lambda i, t: (i, t, 0)),
        out_shape=jax.ShapeDtypeStruct((b, s, HG_V), BF16),
        scratch_shapes=[pltpu.VMEM((nu, ts, HG_DK), F32), pltpu.VMEM((nu, ts, HG_DK), F32),
                        pltpu.VMEM((nu, ts, HG_DK), F32), pltpu.VMEM((nu, ts, HG_DV), F32),
                        pltpu.VMEM((nu, ts, HG_DK), BF16), pltpu.VMEM((nu, ts, HG_DK), BF16),
                        pltpu.VMEM((nu, ts // HG_CHUNK, HG_DK), F32),
                        pltpu.VMEM((nu, ts, HG_DV), F32),
                        pltpu.VMEM((nu, HG_DV, HG_DK), F32)],
        compiler_params=_cparams(("parallel", "arbitrary")),
        name="hgrn2",
    )(proj, proj, proj, proj, lb.reshape(1, HG_QK), out_norm.reshape(1, HG_DV), tmat)


def _t5_band_buckets():
    t = np.arange(WINDOW)[:, None]
    s = np.arange(2 * WINDOW)[None, :]
    dist = t + WINDOW - s
    max_exact = N_BUCKETS // 2
    large = max_exact + (np.log(np.maximum(dist, max_exact) / max_exact) / math.log(MAX_DISTANCE / max_exact)
                         * (N_BUCKETS - max_exact)).astype(np.int32)
    large = np.minimum(large, N_BUCKETS - 1)
    band = (dist >= 0) & (dist < WINDOW)
    return np.where(dist < max_exact, np.maximum(dist, 0), large).astype(np.int32), band


def _swa_kernel(sink_ref, q_ref, kc_ref, kp_ref, vc_ref, vp_ref, bias_ref, qg_ref, kg_ref, gmat_ref, o_ref):
    n = pl.program_id(1)
    w2 = 2 * WINDOW
    no_prev = jnp.where(n > 0, 0.0, -jnp.inf)
    q_groups = SW_HEADS * SW_DIM // LANES
    kv_groups = SW_KV_HEADS * SW_DIM // LANES
    per_kv = q_groups // kv_groups
    kv_rows = (SW_QBLK + 1) * WINDOW
    grp = lambda ref, p: ref[0, :, p * LANES:(p + 1) * LANES]

    def head_norm(x, gain):
        sq = jnp.concatenate(_split_bf16(x * x), axis=1)
        ssq = _dot(sq, gmat_ref[...])
        return x * lax.rsqrt(ssq * (1.0 / SW_DIM) + EPS) * gain

    qn = head_norm(jnp.concatenate([grp(q_ref, p)[i * WINDOW:(i + 1) * WINDOW]
                                    for i in range(SW_QBLK) for p in range(q_groups)], axis=0), qg_ref[...])
    kn = head_norm(jnp.concatenate([x for r in range(kv_groups) for x in (grp(kp_ref, r), grp(kc_ref, r))], axis=0),
                   kg_ref[...])
    v = jnp.concatenate([x for r in range(kv_groups) for x in (grp(vp_ref, r), grp(vc_ref, r))], axis=0)
    kn_rot, v_rot = pltpu.roll(kn, SW_DIM, 1), pltpu.roll(v, SW_DIM, 1)
    kn, kn_rot, v, v_rot = (a.astype(BF16) for a in (kn, kn_rot, v, v_rot))
    ones = jnp.ones((w2, LANES), BF16)

    rows_q = per_kv * WINDOW
    lane = lax.broadcasted_iota(jnp.int32, (rows_q, LANES), 1)
    row = lax.broadcasted_iota(jnp.int32, (rows_q, LANES), 0)
    same_half = (lane < SW_DIM) == (row < rows_q // 2)
    blocks = [(i, r) for i in range(SW_QBLK) for r in range(kv_groups)]
    keys = lambda i, r: slice(r * kv_rows + i * WINDOW, r * kv_rows + i * WINDOW + w2)

    logits = {}
    for i, r in blocks:
        qr = qn[(i * kv_groups + r) * rows_q:(i * kv_groups + r + 1) * rows_q]
        zero = jnp.zeros_like(qr)
        logits[i, r] = (_dot_nt(jnp.where(same_half, qr, zero).astype(BF16), kn[keys(i, r)]),
                        _dot_nt(jnp.where(same_half, zero, qr).astype(BF16), kn_rot[keys(i, r)]))

    weights, sink_terms = {}, {}
    for i, r in blocks:
        w_r, s_r = ([], []), ([], [])
        for j in range(per_kv):
            p = r * per_kv + j
            second = j >= per_kv // 2
            for variant, h in ((0, 2 * p + int(second)), (1, 2 * p + int(not second))):
                lg = logits[i, r][variant][j * WINDOW:(j + 1) * WINDOW]
                bias_prev = bias_ref[h, :, :WINDOW]
                lp = lg[:, :WINDOW] + (bias_prev + no_prev if i == 0 else bias_prev)
                lc = lg[:, WINDOW:] + bias_ref[h, :, WINDOW:]
                sink = sink_ref[h]
                m = jnp.maximum(jnp.max(jnp.maximum(lp, lc), axis=-1, keepdims=True), sink)
                w_r[variant].append(jnp.concatenate([jnp.exp2(lp - m), jnp.exp2(lc - m)], axis=1).astype(BF16))
                s_r[variant].append(jnp.broadcast_to(jnp.exp2(sink - m), (WINDOW, LANES)))
        weights[i, r] = tuple(jnp.concatenate(x, axis=0) for x in w_r)
        sink_terms[i, r] = tuple(jnp.concatenate(x, axis=0) for x in s_r)

    outs = {(i, r): (_dot(weights[i, r][0], jnp.concatenate([v[keys(i, r)], ones], axis=1)),
                     _dot(weights[i, r][1], jnp.concatenate([v_rot[keys(i, r)], ones], axis=1)))
            for i, r in blocks}
    for i, r in blocks:
        o = [outs[i, r][t][:, :LANES] / (outs[i, r][t][:, LANES:] + sink_terms[i, r][t]) for t in range(2)]
        o = jnp.where(same_half, o[0], o[1]).astype(o_ref.dtype)
        for j in range(per_kv):
            p = r * per_kv + j
            o_ref[0, i * WINDOW:(i + 1) * WINDOW, p * LANES:(p + 1) * LANES] = o[j * WINDOW:(j + 1) * WINDOW]


def _swa_group_sum():
    i = np.arange(2 * LANES)[:, None] % LANES
    j = np.arange(LANES)[None, :]
    return (i // SW_DIM == j // SW_DIM).astype(np.float32)


def sliding_window(proj, q_norm, k_norm, sinks, rel_bias):
    b, s, _ = proj.shape
    bucket, band = _t5_band_buckets()
    onehot = jnp.asarray(bucket[..., None] == np.arange(N_BUCKETS), F32)
    bias = jnp.einsum('tsb,bh->hts', onehot, rel_bias.astype(F32), precision=lax.Precision.HIGHEST)
    bias = jnp.where(band, bias * LOG2E, -jnp.inf)
    kv_w = SW_KV_HEADS * SW_DIM
    q_w = SW_HEADS * SW_DIM
    k_blk = q_w // kv_w
    per_lane_group = LANES // SW_DIM
    q_gain = jnp.tile(q_norm.astype(F32), per_lane_group).reshape(1, LANES) * (SW_DIM ** -0.5 * LOG2E)
    k_gain = jnp.tile(k_norm.astype(F32), per_lane_group).reshape(1, LANES)
    tq = SW_QBLK * WINDOW
    assert s % tq == 0
    prev = lambda n: jnp.maximum(SW_QBLK * n - 1, 0)
    const2 = lambda i, n: (0, 0)
    return pl.pallas_call(
        _swa_kernel,
        grid=(b, s // tq),
        in_specs=[pl.BlockSpec(memory_space=pltpu.SMEM),
                  pl.BlockSpec((1, tq, q_w), lambda i, n: (i, n, 0)),
                  pl.BlockSpec((1, tq, kv_w), lambda i, n: (i, n, k_blk)),
                  pl.BlockSpec((1, WINDOW, kv_w), lambda i, n: (i, prev(n), k_blk)),
                  pl.BlockSpec((1, tq, kv_w), lambda i, n: (i, n, k_blk + 1)),
                  pl.BlockSpec((1, WINDOW, kv_w), lambda i, n: (i, prev(n), k_blk + 1)),
                  pl.BlockSpec((SW_HEADS, WINDOW, 2 * WINDOW), lambda i, n: (0, 0, 0)),
                  pl.BlockSpec((1, LANES), const2),
                  pl.BlockSpec((1, LANES), const2),
                  pl.BlockSpec((2 * LANES, LANES), const2)],
        out_specs=pl.BlockSpec((1, tq, q_w), lambda i, n: (i, n, 0)),
        out_shape=jax.ShapeDtypeStruct((b, s, q_w), BF16),
        compiler_params=_cparams(("parallel", "parallel")),
        name="sliding_window",
    )(sinks.astype(F32) * LOG2E, proj, proj, proj, proj, proj, bias, q_gain, k_gain,
      jnp.asarray(_swa_group_sum(), BF16))


def kernel(x, p, mix_norm, ab_w_in, hg_lb_logits, hg_out_norm, ab_w_out, c_w_in, q_norm, k_norm, sinks,
           rel_bias, c_w_out, ffn_norm, ffn_up, ffn_conv, ffn_conv_b, ffn_down, ple_norm, ple_gate, ple_proj):
    b, s, d = x.shape
    depth = p.shape[0]
    m = b * s
    lb_cum = jnp.cumsum(jax.nn.softmax(hg_lb_logits.astype(F32), axis=0), axis=0)
    lower_bounds = lb_cum - lb_cum[0]
    ab_w_in, ab_w_out, c_w_in, c_w_out, ffn_up, ffn_down, ple_gate, ple_proj = (
        w.astype(BF16) for w in (ab_w_in, ab_w_out, c_w_in, c_w_out, ffn_up, ffn_down, ple_gate, ple_proj))
    conv_b = ffn_conv_b.reshape(depth, 1, -1)
    p = p.reshape(depth, m, -1)

    def in_proj(i):
        if i % 2 == 0:
            return mix_norm[i], ab_w_in, i // 2, (3 * SB_WIDTH, 2 * HG_QK + 2 * HG_V), (BF16, F32)
        return mix_norm[i], c_w_in, i // 2, (c_w_in.shape[2],), (F32,)

    h = x.reshape(m, d)
    proj = norm_proj(h, *in_proj(0))
    for i in range(depth):
        j = i // 2
        if i % 2 == 0:
            att, gates = proj
            o_a = stick_breaking(att.reshape(b, s, -1))
            o_b = hgrn2(gates.reshape(b, s, -1), lower_bounds[j], hg_out_norm[j])
            parts, w_out = [o_a.reshape(m, -1), o_b.reshape(m, -1)], ab_w_out
        else:
            (qkv,) = proj
            o = sliding_window(qkv.reshape(b, s, -1), q_norm[j], k_norm[j], sinks[j], rel_bias)
            parts, w_out = [o.reshape(m, -1)], c_w_out
        h = mix_ffn(parts, w_out, j, h, ffn_norm[i], ffn_up, ffn_conv, conv_b, ffn_down, i, s)
        h, *proj = ple_and_proj(h, ple_norm[i], ple_gate, p, ple_proj, i, in_proj(i + 1) if i + 1 < depth else None)
    return h.reshape(b, s, d)
```
